```python
import math
import jax
import jax.numpy as jnp
from jax import lax
import numpy as np

D_MODEL = 1024
BATCH = 4
SEQ = 8192
DEPTH = 4

GRID_W = 64
CTX_LEN = 256
N_MIXERS = 3
N_CONV_LAYERS = (DEPTH + 2) // 3
N_SSM_LAYERS = (DEPTH + 1) // 3
N_ATTN_LAYERS = DEPTH // 3

DEEPNORM_ALPHA = (2 * DEPTH) ** 0.25
DEEPNORM_BETA = (8 * DEPTH) ** -0.25
LN_EPS = 1e-5
RMS_EPS = 1e-6

CONV_WIDTH = 31

SSM_D_INNER = 2 * D_MODEL
SSM_HEAD_DIM = 64
SSM_HEADS = SSM_D_INNER // SSM_HEAD_DIM
SSM_GROUPS = 8
SSM_HPG = SSM_HEADS // SSM_GROUPS
SSM_STATE = 128
SSM_CONV_WIDTH = 5
SSM_CHUNK = 128
SSM_BC = SSM_GROUPS * SSM_STATE
SSM_XBC = SSM_D_INNER + 2 * SSM_BC
SSM_IN = SSM_D_INNER + SSM_XBC + 2 * SSM_HEADS

ATTN_HEAD_DIM = 64
ATTN_HEADS = D_MODEL // ATTN_HEAD_DIM
ATTN_KV_HEADS = 4
ATTN_Q_PER_KV = ATTN_HEADS // ATTN_KV_HEADS
ATTN_QKV = (ATTN_HEADS + 2 * ATTN_KV_HEADS) * ATTN_HEAD_DIM
ATTN_BLOCK = 128
ROPE_THETA = 10000.0

FFN_HIDDEN = -(-8 * D_MODEL // (3 * 256)) * 256

F32 = jnp.float32

kernel_name = "hybrid_conv_ssd_gqa_diffusion_trunk"


def layer_norm(x, g, b):
    xf = x.astype(F32)
    mu = jnp.mean(xf, axis=-1, keepdims=True)
    var = jnp.mean(jnp.square(xf - mu), axis=-1, keepdims=True)
    return ((xf - mu) * lax.rsqrt(var + LN_EPS) * g + b).astype(x.dtype)


def rms_norm(x, g):
    xf = x.astype(F32)
    y = xf * lax.rsqrt(jnp.mean(xf * xf, axis=-1, keepdims=True) + RMS_EPS) * g
    return y.astype(x.dtype)


def ada_modulation(cond, w, b):
    m = jax.nn.silu(cond) @ w + b
    if m.ndim == 2:
        m = m[:, None, :]
    return jnp.split(m, 6, axis=-1)


def modulate(h, shift, scale):
    return h * (1.0 + scale) + shift


def post_norm_residual(h, y, gate, g, b):
    return layer_norm(DEEPNORM_ALPHA * h + gate * y, g, b)


def depthwise_conv(x, w, b):
    k = w.shape[0]
    y = lax.conv_general_dilated(
        x, w[:, None, :], window_strides=(1,), padding=[((k - 1) // 2, k // 2)],
        dimension_numbers=("NWC", "WIO", "NWC"), feature_group_count=x.shape[-1])
    return y + b


def swiglu(h, w1, w3, w2):
    return (jax.nn.silu(h @ w1) * (h @ w3)) @ w2


def conformer_conv(u, pw1_w, pw1_b, dw_w, dw_b, n_g, n_b, pw2_w, pw2_b):
    a, g = jnp.split(u @ pw1_w + pw1_b, 2, axis=-1)
    v = a * jax.nn.sigmoid(g)
    v = depthwise_conv(v, dw_w, dw_b)
    v = jax.nn.silu(layer_norm(v, n_g, n_b))
    return v @ pw2_w + pw2_b


def conv_mixer(u_ctx, u_lat, pw1_w, pw1_b, dw_w, dw_b, n_g, n_b, pw2_w, pw2_b, ctx_out):
    y_lat = conformer_conv(u_lat, pw1_w, pw1_b, dw_w, dw_b, n_g, n_b, pw2_w, pw2_b)
    y_ctx = conformer_conv(u_ctx, pw1_w, pw1_b, dw_w, dw_b, n_g, n_b, pw2_w, pw2_b) if ctx_out else None
    return y_ctx, y_lat


def ssd_chunk_scan(x, dt, a_log, bm, cm, h0):
    b, l = x.shape[:2]
    nc = l // SSM_CHUNK

    def chunks(t):
        t = t.astype(F32).reshape((b, nc, SSM_CHUNK) + t.shape[2:])
        return jnp.moveaxis(t, 1, 0)

    a = -jnp.exp(a_log.astype(F32))
    tri = jnp.tril(jnp.ones((SSM_CHUNK, SSM_CHUNK), dtype=bool))[None, :, :, None, None]

    def step(h, inp):
        xc, dtc, bc, cc = inp
        cum = jnp.cumsum(dtc * a, axis=1)
        seg = jnp.exp(jnp.where(tri, cum[:, :, None] - cum[:, None, :], -jnp.inf))
        cb = jnp.einsum("bign,bjgn->bijg", cc, bc)
        y = jnp.einsum("bijgr,bjgrp->bigrp", cb[..., None] * seg, dtc[..., None] * xc)
        y = y + jnp.einsum("bign,bgrpn->bigrp", cc, h) * jnp.exp(cum)[..., None]
        w_end = (jnp.exp(cum[:, -1:] - cum) * dtc)[..., None] * xc
        h = h * jnp.exp(cum[:, -1])[..., None, None] + jnp.einsum("bjgn,bjgrp->bgrpn", bc, w_end)
        return h, y

    h, ys = lax.scan(step, h0.astype(F32), (chunks(x), chunks(dt), chunks(bm), chunks(cm)))
    return jnp.moveaxis(ys, 0, 1).reshape(x.shape), h


def ssm_mixer(u_ctx, u_lat, in_w, conv_w, conv_b, a_log, dt_bias, d_skip, norm_g, out_w, ctx_out):
    a_log = a_log.reshape(2, SSM_GROUPS, SSM_HPG)
    dt_bias = dt_bias.reshape(2, SSM_GROUPS, SSM_HPG)
    d_skip = d_skip.reshape(SSM_GROUPS, SSM_HPG)

    def project(u):
        b, l = u.shape[:2]
        z, xbc, dt = jnp.split(u @ in_w, [SSM_D_INNER, SSM_D_INNER + SSM_XBC], axis=-1)
        xbc = jax.nn.silu(depthwise_conv(xbc, conv_w, conv_b))
        xs, bm, cm = jnp.split(xbc, [SSM_D_INNER, SSM_D_INNER + SSM_BC], axis=-1)
        xs = xs.reshape(b, l, SSM_GROUPS, SSM_HPG, SSM_HEAD_DIM)
        bm = bm.reshape(b, l, SSM_GROUPS, SSM_STATE)
        cm = cm.reshape(b, l, SSM_GROUPS, SSM_STATE)
        dt = jax.nn.softplus(dt.astype(F32).reshape(b, l, 2, SSM_GROUPS, SSM_HPG) + dt_bias)
        return z, xs, bm, cm, dt

    def rev(t):
        return jnp.flip(t, axis=1)

    def bidir(xs, bm, cm, dt, h0_f, h0_b):
        y_f, h_f = ssd_chunk_scan(xs, dt[:, :, 0], a_log[0], bm, cm, h0_f)
        y_b, h_b = ssd_chunk_scan(rev(xs), rev(dt[:, :, 1]), a_log[1], rev(bm), rev(cm), h0_b)
        return y_f + rev(y_b), h_f, h_b

    def finish(y, xs, z):
        y = (y + d_skip[..., None] * xs.astype(F32)).reshape(z.shape)
        g = y * jax.nn.silu(z.astype(F32))
        g = g.reshape(z.shape[:-1] + (SSM_GROUPS, -1))
        g = g * lax.rsqrt(jnp.mean(g * g, axis=-1, keepdims=True) + RMS_EPS)
        g = g.reshape(z.shape) * norm_g
        return g.astype(z.dtype) @ out_w

    zc, xc, bc, cc, dtc = project(u_ctx)
    h0 = jnp.zeros((u_ctx.shape[0], SSM_GROUPS, SSM_HPG, SSM_HEAD_DIM, SSM_STATE), F32)
    yc, hc_f, hc_b = bidir(xc, bc, cc, dtc, h0, h0)
    zl, xl, bl, cl, dtl = project(u_lat)
    yl, _, _ = bidir(xl, bl, cl, dtl, hc_f, hc_b)
    y_lat = finish(yl, xl, zl)
    y_ctx = finish(yc, xc, zc) if ctx_out else None
    return y_ctx, y_lat


def axial_rope(n_tok):
    rows = n_tok // GRID_W
    row = jnp.repeat(jnp.arange(rows, dtype=F32), GRID_W)
    col = jnp.tile(jnp.arange(GRID_W, dtype=F32), rows)
    axis_dim = ATTN_HEAD_DIM // 2
    inv_freq = ROPE_THETA ** (-jnp.arange(0, axis_dim, 2, dtype=F32) / axis_dim)
    ang = jnp.concatenate([row[:, None] * inv_freq, col[:, None] * inv_freq], axis=-1)
    return jnp.cos(ang), jnp.sin(ang)


def apply_rope(x, cos, sin):
    xf = x.astype(F32).reshape(x.shape[:-1] + (-1, 2))
    x1, x2 = xf[..., 0], xf[..., 1]
    c, s = cos[None, :, None, :], sin[None, :, None, :]
    return jnp.stack([x1 * c - x2 * s, x1 * s + x2 * c], axis=-1).reshape(x.shape).astype(x.dtype)


def gqa_attend(q, k, v):
    b, lq = q.shape[:2]
    qg = q.reshape(b, lq, ATTN_KV_HEADS, ATTN_Q_PER_KV, ATTN_HEAD_DIM)
    s = jnp.einsum("bqkrd,bskd->bkrqs", qg, k).astype(F32) * (ATTN_HEAD_DIM ** -0.5)
    p = jax.nn.softmax(s, axis=-1)
    o = jnp.einsum("bkrqs,bskd->bqkrd", p.astype(v.dtype), v)
    return o.reshape(b, lq, ATTN_HEADS * ATTN_HEAD_DIM)


def blocked_attend(q, k, v):
    b, l = q.shape[:2]
    nb = l // ATTN_BLOCK
    qb = jnp.moveaxis(q.reshape(b, nb, ATTN_BLOCK, ATTN_HEADS, ATTN_HEAD_DIM), 1, 0)
    ob = lax.map(lambda qi: gqa_attend(qi, k, v), qb)
    return jnp.moveaxis(ob, 0, 1).reshape(b, l, ATTN_HEADS * ATTN_HEAD_DIM)


def attn_mixer(u_ctx, u_lat, qkv_w, q_g, k_g, out_w, ctx_out):
    def qkv(u):
        b, l = u.shape[:2]
        q, k, v = jnp.split(u @ qkv_w, [ATTN_HEADS * ATTN_HEAD_DIM,
                                        (ATTN_HEADS + ATTN_KV_HEADS) * ATTN_HEAD_DIM], axis=-1)
        q = rms_norm(q.reshape(b, l, ATTN_HEADS, ATTN_HEAD_DIM), q_g)
        k = rms_norm(k.reshape(b, l, ATTN_KV_HEADS, ATTN_HEAD_DIM), k_g)
        return q, k, v.reshape(b, l, ATTN_KV_HEADS, ATTN_HEAD_DIM)

    qc, kc, vc = qkv(u_ctx)
    ql, kl, vl = qkv(u_lat)
    cos, sin = axial_rope(u_lat.shape[1])
    ql, kl = apply_rope(ql, cos, sin), apply_rope(kl, cos, sin)
    k_all = jnp.concatenate([kc, kl], axis=1)
    v_all = jnp.concatenate([vc, vl], axis=1)
    y_lat = blocked_attend(ql, k_all, v_all) @ out_w
    y_ctx = gqa_attend(qc, kc, vc) @ out_w if ctx_out else None
    return y_ctx, y_lat


def setup_inputs(seed: int = 0) -> dict:
    key = jax.random.key(seed)
    keys = iter(jax.random.split(key, 40))

    def nrm(shape, scale):
        return jax.random.normal(next(keys), shape, F32) * scale

    def gain(shape):
        return 1.0 + nrm(shape, 0.02)

    D, L = D_MODEL, DEPTH
    nc, ns, na = N_CONV_LAYERS, N_SSM_LAYERS, N_ATTN_LAYERS
    beta = DEEPNORM_BETA
    dt0 = jnp.exp(jax.random.uniform(next(keys), (ns, 2 * SSM_HEADS), F32,
                                     math.log(1e-3), math.log(1e-1)))
    return {
        "x": nrm((BATCH, SEQ, D), 1.0),
        "c": nrm((BATCH, D), 1.0),
        "ctx": nrm((BATCH, CTX_LEN, D), 1.0),
        "c_ctx": nrm((D,), 1.0),
        "ada_w": nrm((L, D, 6 * D), 0.5 * D ** -0.5),
        "ada_b": nrm((L, 6 * D), 0.02),
        "ln_mix_g": gain((L, D)),
        "ln_mix_b": nrm((L, D), 0.02),
        "ln_ffn_g": gain((L, D)),
        "ln_ffn_b": nrm((L, D), 0.02),
        "ffn_w1": nrm((L, D, FFN_HIDDEN), D ** -0.5),
        "ffn_w3": nrm((L, D, FFN_HIDDEN), D ** -0.5),
        "ffn_w2": nrm((L, FFN_HIDDEN, D), beta * FFN_HIDDEN ** -0.5),
        "conv_pw1_w": nrm((nc, D, 2 * D), D ** -0.5),
        "conv_pw1_b": nrm((nc, 2 * D), 0.02),
        "conv_dw_w": nrm((nc, CONV_WIDTH, D), CONV_WIDTH ** -0.5),
        "conv_dw_b": nrm((nc, D), 0.02),
        "conv_norm_g": gain((nc, D)),
        "conv_norm_b": nrm((nc, D), 0.02),
        "conv_pw2_w": nrm((nc, D, D), beta * D ** -0.5),
        "conv_pw2_b": nrm((nc, D), 0.02),
        "ssm_in_w": nrm((ns, D, SSM_IN), D ** -0.5),
        "ssm_conv_w": nrm((ns, SSM_CONV_WIDTH, SSM_XBC), SSM_CONV_WIDTH ** -0.5),
        "ssm_conv_b": nrm((ns, SSM_XBC), 0.02),
        "ssm_a_log": jnp.log(jax.random.uniform(next(keys), (ns, 2 * SSM_HEADS), F32, 1.0, 16.0)),
        "ssm_dt_bias": dt0 + jnp.log(-jnp.expm1(-dt0)),
        "ssm_d": 1.0 + nrm((ns, SSM_HEADS), 0.1),
        "ssm_norm_g": gain((ns, SSM_D_INNER)),
        "ssm_out_w": nrm((ns, SSM_D_INNER, D), beta * SSM_D_INNER ** -0.5),
        "attn_qkv_w": nrm((na, D, ATTN_QKV), D ** -0.5),
        "attn_q_norm": gain((na, ATTN_HEAD_DIM)),
        "attn_k_norm": gain((na, ATTN_HEAD_DIM)),
        "attn_out_w": nrm((na, D, D), beta * D ** -0.5),
    }


def reference(x, c, ctx, c_ctx, ada_w, ada_b, ln_mix_g, ln_mix_b, ln_ffn_g, ln_ffn_b,
              ffn_w1, ffn_w3, ffn_w2,
              conv_pw1_w, conv_pw1_b, conv_dw_w, conv_dw_b, conv_norm_g, conv_norm_b,
              conv_pw2_w, conv_pw2_b,
              ssm_in_w, ssm_conv_w, ssm_conv_b, ssm_a_log, ssm_dt_bias, ssm_d, ssm_norm_g, ssm_out_w,
              attn_qkv_w, attn_q_norm, attn_k_norm, attn_out_w):
    h_lat, h_ctx = x, ctx
    for i in range(DEPTH):
        last = i == DEPTH - 1
        kind, j = i % N_MIXERS, i // N_MIXERS
        ctx_in = (not last) or kind != 0
        m_lat = ada_modulation(c, ada_w[i], ada_b[i])
        u_lat = modulate(h_lat, m_lat[0], m_lat[1])
        if ctx_in:
            m_ctx = ada_modulation(c_ctx, ada_w[i], ada_b[i])
            u_ctx = modulate(h_ctx, m_ctx[0], m_ctx[1])
        else:
            u_ctx = None
        if kind == 0:
            y_ctx, y_lat = conv_mixer(u_ctx, u_lat, conv_pw1_w[j], conv_pw1_b[j], conv_dw_w[j],
                                      conv_dw_b[j], conv_norm_g[j], conv_norm_b[j],
                                      conv_pw2_w[j], conv_pw2_b[j], not last)
        elif kind == 1:
            y_ctx, y_lat = ssm_mixer(u_ctx, u_lat, ssm_in_w[j], ssm_conv_w[j], ssm_conv_b[j],
                                     ssm_a_log[j], ssm_dt_bias[j], ssm_d[j], ssm_norm_g[j],
                                     ssm_out_w[j], not last)
        else:
            y_ctx, y_lat = attn_mixer(u_ctx, u_lat, attn_qkv_w[j], attn_q_norm[j], attn_k_norm[j],
                                      attn_out_w[j], not last)
        h_lat = post_norm_residual(h_lat, y_lat, m_lat[2], ln_mix_g[i], ln_mix_b[i])
        f_lat = swiglu(modulate(h_lat, m_lat[3], m_lat[4]), ffn_w1[i], ffn_w3[i], ffn_w2[i])
        h_lat = post_norm_residual(h_lat, f_lat, m_lat[5], ln_ffn_g[i], ln_ffn_b[i])
        if not last:
            h_ctx = post_norm_residual(h_ctx, y_ctx, m_ctx[2], ln_mix_g[i], ln_mix_b[i])
            f_ctx = swiglu(modulate(h_ctx, m_ctx[3], m_ctx[4]), ffn_w1[i], ffn_w3[i], ffn_w2[i])
            h_ctx = post_norm_residual(h_ctx, f_ctx, m_ctx[5], ln_ffn_g[i], ln_ffn_b[i])
    return h_lat
```

```python
import functools
import math

import jax
import jax.numpy as jnp
from jax import lax
from jax.experimental import pallas as pl
from jax.experimental.pallas import tpu as pltpu

F32 = jnp.float32
BF16 = jnp.bfloat16

DEPTH = 4
GRID_W = 64
N_MIXERS = 3
DEEPNORM_ALPHA = (2 * DEPTH) ** 0.25
LN_EPS = 1e-5
RMS_EPS = 1e-6
CONV_WIDTH = 31
SSM_HEAD_DIM = 64
SSM_GROUPS = 8
SSM_STATE = 128
SSM_CONV_WIDTH = 5
SSM_CHUNK = 128
ATTN_HEAD_DIM = 64
ATTN_KV_HEADS = 4
ROPE_THETA = 10000.0

LANES = 128
BF16_ROWS = 16
VMEM_LIMIT = 56 * 1024 * 1024

MOD_ROWS = 8
NEG_BIG = -1e30


def _dot(a, b):
    return jnp.dot(a, b, preferred_element_type=F32)


def _dot_nt(a, b):
    return lax.dot_general(a, b, (((1,), (1,)), ((), ())), preferred_element_type=F32)


def _sigmoid(x):
    return 1.0 / (1.0 + jnp.exp(-x))


def _silu(x):
    return x * _sigmoid(x)


def _layer_norm(z, g, b):
    mu = jnp.mean(z, axis=-1, keepdims=True)
    d = z - mu
    var = jnp.mean(d * d, axis=-1, keepdims=True)
    return d * lax.rsqrt(var + LN_EPS) * g + b


def _split3(a):
    a1 = a.astype(BF16)
    r1 = a - a1.astype(F32)
    a2 = r1.astype(BF16)
    a3 = (r1 - a2.astype(F32)).astype(BF16)
    return a1, a2, a3


def _const_spec(shape):
    nd = len(shape)
    return pl.BlockSpec(shape, lambda *_: (0,) * nd, pipeline_mode=pl.Buffered(1))


def _params(n_grid_axes, sem=None):
    return pltpu.CompilerParams(
        dimension_semantics=sem or ("parallel",) * n_grid_axes,
        vmem_limit_bytes=VMEM_LIMIT)


class _Geom:
    def __init__(self, batch, seq, ctx_len, d_model):
        self.B, self.L, self.C, self.D = batch, seq, ctx_len, d_model
        self.rows_lat = batch * seq
        self.rows = batch * (seq + ctx_len)

    def mod_spec(self, layer, which, tm):
        n_lat_tiles = self.rows_lat // tm
        tiles_per_seq = self.L // tm
        b = self.B

        def imap(t, *_):
            row = jnp.where(t < n_lat_tiles, t // tiles_per_seq, b)
            return ((layer * MOD_ROWS + row) * 6 + which, 0, 0)

        return pl.BlockSpec((None, 1, self.D), imap)


def _mod_kernel(cond_ref, w_ref, b_ref, o_ref):
    s = _silu(cond_ref[...])
    w = w_ref[...]
    s_hi = s.astype(BF16)
    s_lo = (s - s_hi.astype(F32)).astype(BF16)
    w_hi = w.astype(BF16)
    w_lo = (w - w_hi.astype(F32)).astype(BF16)
    acc = _dot(s_hi, w_hi) + _dot(s_lo, w_hi) + _dot(s_hi, w_lo)
    o_ref[...] = acc + b_ref[...]


def _modulation_table(cond, ada_w, ada_b):
    depth, d, n = ada_w.shape
    tn = n // 4
    out = pl.pallas_call(
        _mod_kernel,
        grid=(depth, n // tn),
        in_specs=[
            pl.BlockSpec((MOD_ROWS, d), lambda l, j: (0, 0)),
            pl.BlockSpec((None, d, tn), lambda l, j: (l, 0, j)),
            pl.BlockSpec((None, 1, tn), lambda l, j: (l, 0, j)),
        ],
        out_specs=pl.BlockSpec((None, MOD_ROWS, tn), lambda l, j: (l, 0, j)),
        out_shape=jax.ShapeDtypeStruct((depth, MOD_ROWS, n), F32),
        compiler_params=_params(2),
        name="mod_table",
    )(cond, ada_w, ada_b.reshape(depth, 1, n))
    return out.reshape(depth * MOD_ROWS * 6, 1, d)


def _ffn_kernel(h_ref, sh_ref, sc_ref, gt_ref, w1_ref, w3_ref, w2_ref, g_ref, b_ref, o_ref,
                *, chunk):
    h = h_ref[...]
    u = (h * (1.0 + sc_ref[...]) + sh_ref[...]).astype(BF16)
    hidden = w1_ref.shape[1]
    acc = jnp.zeros(h.shape, F32)
    for c in range(0, hidden, chunk):
        a = _dot(u, w1_ref[:, c:c + chunk])
        b = _dot(u, w3_ref[:, c:c + chunk])
        g = (_silu(a) * b).astype(BF16)
        acc = acc + _dot(g, w2_ref[c:c + chunk, :])
    z = DEEPNORM_ALPHA * h + gt_ref[...] * acc
    o_ref[...] = _layer_norm(z, g_ref[...], b_ref[...])


def _ffn(geom, layer, h, mods, w1, w3, w2, ln_g, ln_b, n_rows, tm):
    d = geom.D
    hidden = w1.shape[1]
    row = pl.BlockSpec((tm, d), lambda t: (t, 0))
    return pl.pallas_call(
        functools.partial(_ffn_kernel, chunk=2 * LANES),
        grid=(n_rows // tm,),
        in_specs=[row, geom.mod_spec(layer, 3, tm), geom.mod_spec(layer, 4, tm),
                  geom.mod_spec(layer, 5, tm),
                  _const_spec((d, hidden)), _const_spec((d, hidden)), _const_spec((hidden, d)),
                  _const_spec((1, d)), _const_spec((1, d))],
        out_specs=row,
        out_shape=jax.ShapeDtypeStruct((n_rows, d), F32),
        compiler_params=_params(1),
        name="ffn",
    )(h, mods, mods, mods, w1, w3, w2, ln_g, ln_b)


def _proj_res_kernel(x_ref, w_ref, h_ref, gt_ref, g_ref, b_ref, o_ref):
    y = _dot(x_ref[...], w_ref[...])
    z = DEEPNORM_ALPHA * h_ref[...] + gt_ref[...] * y
    o_ref[...] = _layer_norm(z, g_ref[...], b_ref[...])


def _proj_res(geom, layer, x, w, h, mods, ln_g, ln_b, n_rows, tm):
    d = geom.D
    k = x.shape[1]
    return pl.pallas_call(
        _proj_res_kernel,
        grid=(n_rows // tm,),
        in_specs=[pl.BlockSpec((tm, k), lambda t: (t, 0)), _const_spec((k, d)),
                  pl.BlockSpec((tm, d), lambda t: (t, 0)), geom.mod_spec(layer, 2, tm),
                  _const_spec((1, d)), _const_spec((1, d))],
        out_specs=pl.BlockSpec((tm, d), lambda t: (t, 0)),
        out_shape=jax.ShapeDtypeStruct((n_rows, d), F32),
        compiler_params=_params(1),
        name="proj_res",
    )(x, w, h, mods, ln_g, ln_b)


def _pw1_kernel(h_ref, sh_ref, sc_ref, wa_ref, wg_ref, ba_ref, bg_ref, v_ref):
    u = (h_ref[...] * (1.0 + sc_ref[...]) + sh_ref[...]).astype(BF16)
    a = _dot(u, wa_ref[...]) + ba_ref[...]
    g = _dot(u, wg_ref[...]) + bg_ref[...]
    v_ref[...] = (a * _sigmoid(g)).astype(BF16)


def _pw1(geom, layer, h, mods, w, bias, n_rows, tm):
    d = geom.D
    half = lambda j: pl.BlockSpec((d, d), lambda t: (0, j), pipeline_mode=pl.Buffered(1))
    bhalf = lambda j: pl.BlockSpec((1, d), lambda t: (0, j), pipeline_mode=pl.Buffered(1))
    return pl.pallas_call(
        _pw1_kernel,
        grid=(n_rows // tm,),
        in_specs=[pl.BlockSpec((tm, d), lambda t: (t, 0)),
                  geom.mod_spec(layer, 0, tm), geom.mod_spec(layer, 1, tm),
                  half(0), half(1), bhalf(0), bhalf(1)],
        out_specs=pl.BlockSpec((tm, d), lambda t: (t, 0)),
        out_shape=jax.ShapeDtypeStruct((n_rows, d), BF16),
        compiler_params=_params(1),
        name="conv_pw1",
    )(h, mods, mods, w, w, bias, bias)


def _seq_edges(t, tm, geom):
    n_lat_tiles = geom.rows_lat // tm
    lat_tps = geom.L // tm
    ctx_tps = geom.C // tm
    in_lat = t < n_lat_tiles
    pos = jnp.where(in_lat, t % lat_tps, (t - n_lat_tiles) % ctx_tps)
    tps = jnp.where(in_lat, lat_tps, ctx_tps)
    return pos == 0, pos == tps - 1


def _fill_window(win_ref, prev_ref, cur_ref, next_ref, is_start, is_end, tm):
    halo = BF16_ROWS
    prev = prev_ref[...].astype(F32)
    nxt = next_ref[...].astype(F32)
    win_ref[0:halo, :] = jnp.where(is_start, 0.0, prev)
    win_ref[halo:halo + tm, :] = cur_ref[...].astype(F32)
    win_ref[halo + tm:halo + tm + halo, :] = jnp.where(is_end, 0.0, nxt)


def _depthwise(win_ref, w_ref, width, tm, n_cols, emit, row_block=64):
    first = BF16_ROWS - (width - 1) // 2
    for c in range(0, n_cols, LANES):
        for r in range(0, tm, row_block):
            acc = jnp.zeros((row_block, LANES), F32)
            for k in range(width):
                acc = acc + (win_ref[r + first + k:r + first + k + row_block, c:c + LANES]
                             * w_ref[k:k + 1, c:c + LANES])
            emit(r, c, acc)


def _conv_mix_kernel(prev_ref, cur_ref, next_ref, dww_ref, dwb_ref, ng_ref, nb_ref,
                     w2_ref, b2_ref, h_ref, gt_ref, g_ref, b_ref, o_ref, win_ref, cv_ref,
                     *, geom, tm):
    is_start, is_end = _seq_edges(pl.program_id(0), tm, geom)
    _fill_window(win_ref, prev_ref, cur_ref, next_ref, is_start, is_end, tm)

    def emit(r, c, acc):
        cv_ref[r:r + acc.shape[0], c:c + LANES] = acc + dwb_ref[:, c:c + LANES]

    _depthwise(win_ref, dww_ref, CONV_WIDTH, tm, geom.D, emit)
    v = _silu(_layer_norm(cv_ref[...], ng_ref[...], nb_ref[...])).astype(BF16)
    y = _dot(v, w2_ref[...]) + b2_ref[...]
    z = DEEPNORM_ALPHA * h_ref[...] + gt_ref[...] * y
    o_ref[...] = _layer_norm(z, g_ref[...], b_ref[...])


def _halo_specs(tm, n_cols, n_halo_blocks, col_of=None):
    per = tm // BF16_ROWS
    col = (lambda *a: 0) if col_of is None else col_of
    prev = pl.BlockSpec((BF16_ROWS, n_cols),
                        lambda t, *a: (jnp.maximum(t * per - 1, 0), col(t, *a)))
    cur = pl.BlockSpec((tm, n_cols), lambda t, *a: (t, col(t, *a)))
    nxt = pl.BlockSpec((BF16_ROWS, n_cols),
                       lambda t, *a: (jnp.minimum((t + 1) * per, n_halo_blocks - 1), col(t, *a)))
    return prev, cur, nxt


def _conv_mix(geom, layer, v, dw_w, dw_b, n_g, n_b, w2, b2, h, mods, ln_g, ln_b, n_rows, tm):
    d = geom.D
    prev, cur, nxt = _halo_specs(tm, d, v.shape[0] // BF16_ROWS)
    row = pl.BlockSpec((tm, d), lambda t: (t, 0))
    vec = _const_spec((1, d))
    return pl.pallas_call(
        functools.partial(_conv_mix_kernel, geom=geom, tm=tm),
        grid=(n_rows // tm,),
        in_specs=[prev, cur, nxt, _const_spec((CONV_WIDTH, d)), vec, vec, vec,
                  _const_spec((d, d)), vec, row, geom.mod_spec(layer, 2, tm), vec, vec],
        out_specs=row,
        out_shape=jax.ShapeDtypeStruct((n_rows, d), F32),
        scratch_shapes=[pltpu.VMEM((tm + 2 * BF16_ROWS, d), F32), pltpu.VMEM((tm, d), F32)],
        compiler_params=_params(1),
        name="conv_mix",
    )(v, v, v, dw_w, dw_b, n_g, n_b, w2, b2, h, mods, ln_g, ln_b)


def _ssm_in_kernel(h_ref, sh_ref, sc_ref, wz_ref, wx_ref, wdh_ref, wdl_ref, dtb_ref,
                   z_ref, x_ref, dt_ref, *, col_chunk):
    uf = h_ref[...] * (1.0 + sc_ref[...]) + sh_ref[...]
    u = uf.astype(BF16)
    u_lo = (uf - u.astype(F32)).astype(BF16)
    for c in range(0, wz_ref.shape[1], col_chunk):
        z_ref[:, c:c + col_chunk] = _dot(u, wz_ref[:, c:c + col_chunk]).astype(BF16)
    for c in range(0, wx_ref.shape[1], col_chunk):
        x_ref[:, c:c + col_chunk] = _dot(u, wx_ref[:, c:c + col_chunk]).astype(BF16)
    d = _dot(u, wdh_ref[...]) + _dot(u_lo, wdh_ref[...]) + _dot(u, wdl_ref[...]) + dtb_ref[...]
    dt_ref[...] = jnp.maximum(d, 0.0) + jnp.log1p(jnp.exp(-jnp.abs(d)))


def _ssm_in(geom, layer, h, mods, wz, wx, wd_hi, wd_lo, dt_bias, n_rows, tm):
    d = geom.D
    nz, nx = wz.shape[1], wx.shape[1]
    return pl.pallas_call(
        functools.partial(_ssm_in_kernel, col_chunk=8 * LANES),
        grid=(n_rows // tm,),
        in_specs=[pl.BlockSpec((tm, d), lambda t: (t, 0)),
                  geom.mod_spec(layer, 0, tm), geom.mod_spec(layer, 1, tm),
                  _const_spec((d, nz)), _const_spec((d, nx)),
                  _const_spec((d, LANES)), _const_spec((d, LANES)), _const_spec((1, LANES))],
        out_specs=[pl.BlockSpec((tm, nz), lambda t: (t, 0)),
                   pl.BlockSpec((tm, nx), lambda t: (t, 0)),
                   pl.BlockSpec((tm, LANES), lambda t: (t, 0))],
        out_shape=[jax.ShapeDtypeStruct((n_rows, nz), BF16),
                   jax.ShapeDtypeStruct((n_rows, nx), BF16),
                   jax.ShapeDtypeStruct((n_rows, LANES), F32)],
        compiler_params=_params(1),
        name="ssm_in",
    )(h, mods, mods, wz, wx, wd_hi, wd_lo, dt_bias)


def _ssm_conv_kernel(prev_ref, cur_ref, next_ref, w_ref, b_ref, o_ref, win_ref, *, geom, tm):
    is_start, is_end = _seq_edges(pl.program_id(0), tm, geom)
    _fill_window(win_ref, prev_ref, cur_ref, next_ref, is_start, is_end, tm)

    def emit(r, c, acc):
        o_ref[r:r + acc.shape[0], c:c + LANES] = _silu(acc + b_ref[:, c:c + LANES]).astype(BF16)

    _depthwise(win_ref, w_ref, SSM_CONV_WIDTH, tm, o_ref.shape[1], emit)


def _ssm_conv(geom, xbc, w, b, n_rows, tm, col_block):
    n_cols = xbc.shape[1]
    prev, cur, nxt = _halo_specs(tm, col_block, xbc.shape[0] // BF16_ROWS, col_of=lambda t, j: j)
    return pl.pallas_call(
        functools.partial(_ssm_conv_kernel, geom=geom, tm=tm),
        grid=(n_rows // tm, n_cols // col_block),
        in_specs=[prev, cur, nxt,
                  pl.BlockSpec((SSM_CONV_WIDTH, col_block), lambda t, j: (0, j)),
                  pl.BlockSpec((1, col_block), lambda t, j: (0, j))],
        out_specs=pl.BlockSpec((tm, col_block), lambda t, j: (t, j)),
        out_shape=jax.ShapeDtypeStruct((n_rows, n_cols), BF16),
        scratch_shapes=[pltpu.VMEM((tm + 2 * BF16_ROWS, col_block), F32)],
        compiler_params=_params(2),
        name="ssm_conv",
    )(xbc, xbc, xbc, w, b)


def _ssd_direction(x_ref, b_ref, c_ref, dt_ref, dtt_ref, alog_row_ref, alog_col_ref, st_ref,
                   y_ref, *, backward):
    q = SSM_CHUNK
    n_state = SSM_STATE
    hpg = 4
    ii = lax.broadcasted_iota(jnp.int32, (q, q), 0)
    jj = lax.broadcasted_iota(jnp.int32, (q, q), 1)
    mask = (ii <= jj) if backward else (ii >= jj)
    tri = mask.astype(BF16)
    tri_t = ((ii >= jj) if backward else (ii <= jj)).astype(BF16)

    dt = dt_ref[...]
    dtt = dtt_ref[...]
    a_col = dt * (-jnp.exp(alog_row_ref[...]))
    a_row = dtt * (-jnp.exp(alog_col_ref[...]))
    cum_col = sum(_dot(tri, p) for p in _split3(a_col))
    cum_row = sum(_dot(p, tri_t) for p in _split3(a_row))
    last = 0 if backward else q - 1
    e_col = jnp.exp(cum_col)
    cum_last_col = cum_row[:, last:last + 1]
    w_row = jnp.exp(cum_last_col - cum_row) * dtt
    e_last_row = jnp.exp(cum_col[last:last + 1, :])

    lane = lax.broadcasted_iota(jnp.int32, (q, LANES), 1)
    low_half = lane < SSM_HEAD_DIM
    base = 32 * (1 if backward else 0)

    for g in range(SSM_GROUPS):
        bg = b_ref[:, g * n_state:(g + 1) * n_state]
        cg = c_ref[:, g * n_state:(g + 1) * n_state]
        cb = _dot_nt(cg, bg)
        bt = bg.astype(F32).T
        cg32 = cg.astype(F32)
        for pair in range(hpg // 2):
            slot = g * (hpg // 2) + pair
            xp = x_ref[:, slot * LANES:(slot + 1) * LANES]
            hs = st_ref[slot]
            rhs = jnp.concatenate([xp, hs.astype(BF16)], axis=0)
            ys, ss, els = [], [], []
            for r in range(2):
                hl = base + g * hpg + pair * 2 + r
                cc = cum_col[:, hl:hl + 1]
                cr = cum_row[hl:hl + 1, :]
                seg = jnp.exp(jnp.where(mask, cc - cr, NEG_BIG))
                m = (cb * seg * dtt[hl:hl + 1, :]).astype(BF16)
                ce = (cg32 * e_col[:, hl:hl + 1]).astype(BF16)
                ys.append(_dot(jnp.concatenate([m, ce], axis=1), rhs))
                bts = (bt * w_row[hl:hl + 1, :]).astype(BF16)
                ss.append(_dot(bts, xp))
                els.append(e_last_row[:, hl:hl + 1])
            y_ref[:, slot * LANES:(slot + 1) * LANES] = jnp.where(low_half, ys[0], ys[1])
            low_n = low_half[:n_state]
            decay = jnp.where(low_n, els[0], els[1])
            st_ref[slot] = hs * decay + jnp.where(low_n, ss[0], ss[1])


def _ssd_kernel(xf_ref, bf_ref, cf_ref, dtf_ref, dttf_ref,
                xb_ref, bb_ref, cb_ref, dtb_ref, dttb_ref,
                alog_row_ref, alog_col_ref, yf_ref, yb_ref, stf_ref, stb_ref):
    @pl.when(pl.program_id(1) == 0)
    def _():
        stf_ref[...] = jnp.zeros(stf_ref.shape, F32)
        stb_ref[...] = jnp.zeros(stb_ref.shape, F32)

    _ssd_direction(xf_ref, bf_ref, cf_ref, dtf_ref, dttf_ref, alog_row_ref, alog_col_ref,
                   stf_ref, yf_ref, backward=False)
    _ssd_direction(xb_ref, bb_ref, cb_ref, dtb_ref, dttb_ref, alog_row_ref, alog_col_ref,
                   stb_ref, yb_ref, backward=True)


def _ssd(geom, xbc, dt, dtt, alog_row, alog_col, d_inner, with_ctx_out):
    q = SSM_CHUNK
    n_bc = SSM_GROUPS * SSM_STATE
    lat_chunks = geom.L // q
    ctx_chunks = geom.C // q
    lat_total = geom.rows_lat // q
    steps = lat_chunks + ctx_chunks

    def fwd_chunk(b, s):
        return jnp.where(s < ctx_chunks, lat_total + b * ctx_chunks + s,
                         b * lat_chunks + (s - ctx_chunks))

    def bwd_chunk(b, s):
        return jnp.where(s < ctx_chunks, lat_total + b * ctx_chunks + (ctx_chunks - 1 - s),
                         b * lat_chunks + (lat_chunks - 1 - (s - ctx_chunks)))

    def specs(chunk):
        return [pl.BlockSpec((q, d_inner), lambda b, s: (chunk(b, s), 0)),
                pl.BlockSpec((q, n_bc), lambda b, s: (chunk(b, s), d_inner // n_bc)),
                pl.BlockSpec((q, n_bc), lambda b, s: (chunk(b, s), d_inner // n_bc + 1)),
                pl.BlockSpec((q, LANES), lambda b, s: (chunk(b, s), 0)),
                pl.BlockSpec((LANES, q), lambda b, s: (0, chunk(b, s)))]

    n_rows = geom.rows
    y_shape = jax.ShapeDtypeStruct((n_rows, d_inner), F32)
    state = pltpu.VMEM((SSM_GROUPS * 2, SSM_STATE, LANES), F32)
    return pl.pallas_call(
        _ssd_kernel,
        grid=(geom.B, steps),
        in_specs=specs(fwd_chunk) + specs(bwd_chunk)
        + [pl.BlockSpec((1, LANES), lambda b, s: (0, 0)),
           pl.BlockSpec((LANES, 1), lambda b, s: (0, 0))],
        out_specs=[pl.BlockSpec((q, d_inner), lambda b, s: (fwd_chunk(b, s), 0)),
                   pl.BlockSpec((q, d_inner), lambda b, s: (bwd_chunk(b, s), 0))],
        out_shape=[y_shape, y_shape],
        scratch_shapes=[state, state],
        compiler_params=_params(2, ("parallel", "arbitrary")),
        name="ssd_scan",
    )(xbc, xbc, xbc, dt, dtt, xbc, xbc, xbc, dt, dtt, alog_row, alog_col)


def _ssm_out_kernel(yf_ref, yb_ref, x_ref, z_ref, dsk_ref, ng_ref, w_ref, h_ref, gt_ref,
                    g_ref, b_ref, o_ref):
    d_inner = yf_ref.shape[1]
    gw = d_inner // SSM_GROUPS
    acc = jnp.zeros(o_ref.shape, F32)
    for c in range(0, d_inner, gw):
        sl = slice(c, c + gw)
        y = yf_ref[:, sl] + yb_ref[:, sl] + dsk_ref[:, sl] * x_ref[:, sl].astype(F32)
        gt = y * _silu(z_ref[:, sl].astype(F32))
        ms = jnp.mean(gt * gt, axis=-1, keepdims=True)
        gn = (gt * lax.rsqrt(ms + RMS_EPS) * ng_ref[:, sl]).astype(BF16)
        acc = acc + _dot(gn, w_ref[sl, :])
    zz = DEEPNORM_ALPHA * h_ref[...] + gt_ref[...] * acc
    o_ref[...] = _layer_norm(zz, g_ref[...], b_ref[...])


def _ssm_out(geom, layer, yf, yb, xbc, z, d_skip, norm_g, out_w, h, mods, ln_g, ln_b,
             n_rows, tm):
    d = geom.D
    d_inner = z.shape[1]
    wide = pl.BlockSpec((tm, d_inner), lambda t: (t, 0))
    row = pl.BlockSpec((tm, d), lambda t: (t, 0))
    return pl.pallas_call(
        _ssm_out_kernel,
        grid=(n_rows // tm,),
        in_specs=[wide, wide, wide, wide, _const_spec((1, d_inner)), _const_spec((1, d_inner)),
                  _const_spec((d_inner, d)), row, geom.mod_spec(layer, 2, tm),
                  _const_spec((1, d)), _const_spec((1, d))],
        out_specs=row,
        out_shape=jax.ShapeDtypeStruct((n_rows, d), F32),
        compiler_params=_params(1),
        name="ssm_out",
    )(yf, yb, xbc, z, d_skip, norm_g, out_w, h, mods, ln_g, ln_b)


def _qkv_kernel(h_ref, sh_ref, sc_ref, wq_ref, wk_ref, wv_ref, pq_ref, pk_ref, gq_ref, gk_ref,
                cos_ref, sin_ref, q_ref, k_ref, v_ref):
    u = (h_ref[...] * (1.0 + sc_ref[...]) + sh_ref[...]).astype(BF16)
    cos = cos_ref[...]
    sin = sin_ref[...]

    def norm_rope(x, p_ref, g_ref):
        n = x.shape[1]
        ms = _dot((x * x).astype(BF16), p_ref[...])
        xn = x * lax.rsqrt(ms + RMS_EPS) * g_ref[...]
        lane = lax.broadcasted_iota(jnp.int32, xn.shape, 1)
        partner = jnp.where(lane % 2 == 0, pltpu.roll(xn, n - 1, axis=1),
                            pltpu.roll(xn, 1, axis=1))
        reps = n // cos.shape[1]
        return xn * jnp.tile(cos, (1, reps)) + partner * jnp.tile(sin, (1, reps))

    q = norm_rope(_dot(u, wq_ref[...]), pq_ref, gq_ref)
    q_ref[...] = (q * (ATTN_HEAD_DIM ** -0.5)).astype(BF16)
    k_ref[...] = norm_rope(_dot(u, wk_ref[...]), pk_ref, gk_ref).astype(BF16)
    v = _dot(u, wv_ref[...])
    lane = lax.broadcasted_iota(jnp.int32, v.shape, 1)
    v_ref[...] = jnp.where(lane % LANES < ATTN_HEAD_DIM, v, 1.0).astype(BF16)


def _qkv(geom, layer, h, mods, wq, wk2, wv2, pq, pk, gq, gk, cos_t, sin_t, n_rows, tm):
    d = geom.D
    nq, nk = wq.shape[1], wk2.shape[1]
    n_lat_tiles = geom.rows_lat // tm
    tps = geom.L // tm
    table = pl.BlockSpec((tm, LANES), lambda t: (jnp.where(t < n_lat_tiles, t % tps, tps), 0))
    return pl.pallas_call(
        _qkv_kernel,
        grid=(n_rows // tm,),
        in_specs=[pl.BlockSpec((tm, d), lambda t: (t, 0)),
                  geom.mod_spec(layer, 0, tm), geom.mod_spec(layer, 1, tm),
                  _const_spec((d, nq)), _const_spec((d, nk)), _const_spec((d, nk)),
                  _const_spec((nq, nq)), _const_spec((nk, nk)),
                  _const_spec((1, nq)), _const_spec((1, nk)), table, table],
        out_specs=[pl.BlockSpec((tm, nq), lambda t: (t, 0)),
                   pl.BlockSpec((tm, nk), lambda t: (t, 0)),
                   pl.BlockSpec((tm, nk), lambda t: (t, 0))],
        out_shape=[jax.ShapeDtypeStruct((n_rows, nq), BF16),
                   jax.ShapeDtypeStruct((n_rows, nk), BF16),
                   jax.ShapeDtypeStruct((n_rows, nk), BF16)],
        compiler_params=_params(1),
        name="attn_qkv",
    )(h, mods, mods, wq, wk2, wv2, pq, pk, gq, gk, cos_t, sin_t)


def _flash_kernel(q_ref, kl_ref, vl_ref, kc_ref, vc_ref, *rest, n_lat_chunks, ck):
    o_ref = rest[-1]
    tq = q_ref.shape[0]
    lane = lax.broadcasted_iota(jnp.int32, (tq, LANES), 1)
    low_half = lane < ATTN_HEAD_DIM

    def update(qm, kc, vc, m, acc):
        s = _dot_nt(qm, kc)
        m_new = jnp.maximum(m, jnp.max(s, axis=-1, keepdims=True))
        p = jnp.exp(s - m_new).astype(BF16)
        acc = jnp.exp(m - m_new) * acc + _dot(p, vc)
        return m_new, acc

    for pair in range(2):
        q2 = q_ref[:, pair * LANES:(pair + 1) * LANES]
        outs = []
        for r in range(2):
            keep = low_half if r == 0 else jnp.logical_not(low_half)
            qm = jnp.where(keep, q2.astype(F32), 0.0).astype(BF16)
            m = jnp.full((tq, 1), NEG_BIG, F32)
            acc = jnp.zeros((tq, LANES), F32)
            if n_lat_chunks:
                def body(j, carry, qm=qm):
                    off = pl.multiple_of(j * ck, ck)
                    return update(qm, kl_ref[pl.ds(off, ck), :], vl_ref[pl.ds(off, ck), :],
                                  *carry)
                m, acc = lax.fori_loop(0, n_lat_chunks, body, (m, acc))
            m, acc = update(qm, kc_ref[...], vc_ref[...], m, acc)
            denom = pltpu.roll(acc, ATTN_HEAD_DIM, axis=1)
            outs.append(acc * (1.0 / denom))
        o_pair = jnp.where(low_half, outs[0], pltpu.roll(outs[1], ATTN_HEAD_DIM, axis=1))
        o_ref[:, pair * LANES:(pair + 1) * LANES] = o_pair.astype(BF16)


def _flash(geom, q, k2, v2, tq, ck, latent_queries, o_prev=None):
    b, l, c = geom.B, geom.L, geom.C
    gw = 4 * ATTN_HEAD_DIM
    ctx_block0 = geom.rows_lat // c
    if latent_queries:
        nq = l // tq
        q_spec = pl.BlockSpec((tq, gw), lambda bi, g, i: (bi * nq + i, g))
        kl_rows, n_lat_chunks = l, l // ck
        kl_map = lambda bi, g, i: (bi, g)
    else:
        nq = 1
        q_spec = pl.BlockSpec((tq, gw), lambda bi, g, i: (ctx_block0 + bi, g))
        kl_rows, n_lat_chunks = ck, 0
        kl_map = lambda bi, g, i: (0, g)
    kl_spec = pl.BlockSpec((kl_rows, LANES), kl_map)
    kc_spec = pl.BlockSpec((c, LANES), lambda bi, g, i: (ctx_block0 + bi, g))
    args = [q, k2, v2, k2, v2]
    in_specs = [q_spec, kl_spec, kl_spec, kc_spec, kc_spec]
    aliases = {}
    if o_prev is not None:
        args.append(o_prev)
        in_specs.append(pl.BlockSpec(memory_space=pl.ANY))
        aliases = {len(args) - 1: 0}
    return pl.pallas_call(
        functools.partial(_flash_kernel, n_lat_chunks=n_lat_chunks, ck=ck),
        grid=(b, ATTN_KV_HEADS, nq),
        in_specs=in_specs,
        out_specs=q_spec,
        out_shape=jax.ShapeDtypeStruct(q.shape, BF16),
        input_output_aliases=aliases,
        compiler_params=_params(3),
        name="flash_lat" if latent_queries else "flash_ctx",
    )(*args)


def _rope_tables(seq, tm):
    rows = seq // GRID_W
    row = jnp.repeat(jnp.arange(rows, dtype=F32), GRID_W)
    col = jnp.tile(jnp.arange(GRID_W, dtype=F32), rows)
    axis_dim = ATTN_HEAD_DIM // 2
    inv_freq = ROPE_THETA ** (-jnp.arange(0, axis_dim, 2, dtype=F32) / axis_dim)
    ang = jnp.concatenate([row[:, None] * inv_freq, col[:, None] * inv_freq], axis=-1)
    cos = jnp.repeat(jnp.cos(ang), 2, axis=-1)
    sign = jnp.tile(jnp.array([-1.0, 1.0], F32), ATTN_HEAD_DIM // 2)
    sin = jnp.repeat(jnp.sin(ang), 2, axis=-1) * sign
    reps = LANES // ATTN_HEAD_DIM
    cos = jnp.concatenate([jnp.tile(cos, (1, reps)), jnp.ones((tm, LANES), F32)], axis=0)
    sin = jnp.concatenate([jnp.tile(sin, (1, reps)), jnp.zeros((tm, LANES), F32)], axis=0)
    return cos, sin


def _head_mean_matrix(n, head_dim):
    idx = jnp.arange(n) // head_dim
    return jnp.where(idx[:, None] == idx[None, :], 1.0 / head_dim, 0.0).astype(BF16)


def kernel(x, c, ctx, c_ctx, ada_w, ada_b, ln_mix_g, ln_mix_b, ln_ffn_g, ln_ffn_b, ffn_w1, ffn_w3, ffn_w2, conv_pw1_w, conv_pw1_b, conv_dw_w, conv_dw_b, conv_norm_g, conv_norm_b, conv_pw2_w, conv_pw2_b, ssm_in_w, ssm_conv_w, ssm_conv_b, ssm_a_log, ssm_dt_bias, ssm_d, ssm_norm_g, ssm_out_w, attn_qkv_w, attn_q_norm, attn_k_norm, attn_out_w):
    batch, seq, d = x.shape
    ctx_len = ctx.shape[1]
    geom = _Geom(batch, seq, ctx_len, d)
    tm = math.gcd(512, math.gcd(seq, batch * ctx_len))
    tmc = math.gcd(256, math.gcd(seq, ctx_len))
    assert batch < MOD_ROWS and seq % GRID_W == 0 and ctx_len % SSM_CHUNK == 0

    cond = jnp.zeros((MOD_ROWS, d), F32).at[:batch].set(c).at[batch].set(c_ctx)
    mods = _modulation_table(cond, ada_w, ada_b)

    h = jnp.concatenate([x.reshape(batch * seq, d), ctx.reshape(batch * ctx_len, d)], axis=0)
    vec = lambda a: a.reshape(1, -1)

    for i in range(DEPTH):
        last = i == DEPTH - 1
        kind, j = i % N_MIXERS, i // N_MIXERS
        n_rows = geom.rows_lat if (last and kind == 0) else h.shape[0]
        n_rows = min(n_rows, h.shape[0])
        g_mix, b_mix = vec(ln_mix_g[i]), vec(ln_mix_b[i])

        if kind == 0:
            v = _pw1(geom, i, h, mods, conv_pw1_w[j].astype(BF16), vec(conv_pw1_b[j]),
                     n_rows, tm)
            h = _conv_mix(geom, i, v, conv_dw_w[j], vec(conv_dw_b[j]), vec(conv_norm_g[j]),
                          vec(conv_norm_b[j]), conv_pw2_w[j].astype(BF16), vec(conv_pw2_b[j]),
                          h, mods, g_mix, b_mix, n_rows, tmc)
        elif kind == 1:
            in_w = ssm_in_w[j]
            d_inner = ssm_norm_g.shape[1]
            n_xbc = ssm_conv_w.shape[2]
            n_heads2 = ssm_a_log.shape[1]
            wz = in_w[:, :d_inner].astype(BF16)
            wx = in_w[:, d_inner:d_inner + n_xbc].astype(BF16)
            wd = jnp.pad(in_w[:, d_inner + n_xbc:], ((0, 0), (0, LANES - n_heads2)))
            wd_hi = wd.astype(BF16)
            wd_lo = (wd - wd_hi.astype(F32)).astype(BF16)
            dt_bias = jnp.pad(ssm_dt_bias[j], (0, LANES - n_heads2)).reshape(1, LANES)
            z, xbc, dt = _ssm_in(geom, i, h, mods, wz, wx, wd_hi, wd_lo, dt_bias, n_rows, tm)
            xbc = _ssm_conv(geom, xbc, ssm_conv_w[j], vec(ssm_conv_b[j]), n_rows, tmc,
                            8 * LANES)
            a_log = jnp.pad(ssm_a_log[j], (0, LANES - n_heads2), constant_values=NEG_BIG)
            yf, yb = _ssd(geom, xbc, dt, dt.T, a_log.reshape(1, LANES), a_log.reshape(LANES, 1),
                          d_inner, True)
            d_skip = jnp.repeat(ssm_d[j], SSM_HEAD_DIM).reshape(1, d_inner)
            h = _ssm_out(geom, i, yf, yb, xbc, z, d_skip, vec(ssm_norm_g[j]),
                         ssm_out_w[j].astype(BF16), h, mods, g_mix, b_mix, n_rows, tm)
        else:
            w = attn_qkv_w[j]
            hd = ATTN_HEAD_DIM
            n_q = attn_out_w.shape[1]
            n_kv = (w.shape[1] - n_q) // 2
            wq = w[:, :n_q].astype(BF16)
            wk = w[:, n_q:n_q + n_kv].reshape(d, ATTN_KV_HEADS, 1, hd)
            wk2 = jnp.broadcast_to(wk, (d, ATTN_KV_HEADS, 2, hd)).reshape(d, 2 * n_kv)
            wv = w[:, n_q + n_kv:].reshape(d, ATTN_KV_HEADS, 1, hd)
            wv2 = jnp.concatenate([wv, jnp.zeros_like(wv)], axis=2).reshape(d, 2 * n_kv)
            gq = jnp.tile(attn_q_norm[j], n_q // hd).reshape(1, n_q)
            gk = jnp.tile(attn_k_norm[j], 2 * n_kv // hd).reshape(1, 2 * n_kv)
            cos_t, sin_t = _rope_tables(seq, tm)
            q, k2, v2 = _qkv(geom, i, h, mods, wq, wk2.astype(BF16), wv2.astype(BF16),
                             _head_mean_matrix(n_q, hd), _head_mean_matrix(2 * n_kv, hd),
                             gq, gk, cos_t, sin_t, n_rows, tm)
            tq = math.gcd(512, seq)
            ck = math.gcd(512, seq)
            o = _flash(geom, q, k2, v2, tq, ck, True)
            o = _flash(geom, q, k2, v2, ctx_len, ck, False, o_prev=o)
            h = _proj_res(geom, i, o, attn_out_w[j].astype(BF16), h, mods, g_mix, b_mix,
                          n_rows, tm)

        h = _ffn(geom, i, h, mods, ffn_w1[i].astype(BF16), ffn_w3[i].astype(BF16),
                 ffn_w2[i].astype(BF16), vec(ln_ffn_g[i]), vec(ln_ffn_b[i]), n_rows, tm)

    return h[:geom.rows_lat].reshape(batch, seq, d)
```

```python
import functools
import math

import jax
import jax.numpy as jnp
from jax import lax
from jax.experimental import pallas as pl
from jax.experimental.pallas import tpu as pltpu

F32 = jnp.float32
BF16 = jnp.bfloat16

DEPTH = 4
GRID_W = 64
N_MIXERS = 3
DEEPNORM_ALPHA = (2 * DEPTH) ** 0.25
LN_EPS = 1e-5
RMS_EPS = 1e-6
CONV_WIDTH = 31
SSM_HEAD_DIM = 64
SSM_GROUPS = 8
SSM_STATE = 128
SSM_CONV_WIDTH = 5
SSM_CHUNK = 128
ATTN_HEAD_DIM = 64
ATTN_KV_HEADS = 4
ROPE_THETA = 10000.0

LANES = 128
SUBLANES = 8
BF16_ROWS = 16
VMEM_LIMIT = 56 * 1024 * 1024

MOD_ROWS = 8
NEG_BIG = -1e30


def _dot(a, b):
    return jnp.dot(a, b, preferred_element_type=F32)


def _dot_nt(a, b):
    return lax.dot_general(a, b, (((1,), (1,)), ((), ())), preferred_element_type=F32)


def _sigmoid(x):
    return 1.0 / (1.0 + jnp.exp(-x))


def _silu(x):
    return x * _sigmoid(x)


def _layer_norm(z, g, b):
    mu = jnp.mean(z, axis=-1, keepdims=True)
    d = z - mu
    var = jnp.mean(d * d, axis=-1, keepdims=True)
    return d * lax.rsqrt(var + LN_EPS) * g + b


def _split3(a):
    a1 = a.astype(BF16)
    r1 = a - a1.astype(F32)
    a2 = r1.astype(BF16)
    a3 = (r1 - a2.astype(F32)).astype(BF16)
    return a1, a2, a3


def _const_spec(shape):
    nd = len(shape)
    return pl.BlockSpec(shape, lambda *_: (0,) * nd, pipeline_mode=pl.Buffered(1))


def _params(n_grid_axes, sem=None):
    return pltpu.CompilerParams(
        dimension_semantics=sem or ("parallel",) * n_grid_axes,
        vmem_limit_bytes=VMEM_LIMIT)


class _Geom:
    def __init__(self, batch, seq, ctx_len, d_model):
        self.B, self.L, self.C, self.D = batch, seq, ctx_len, d_model
        self.rows_lat = batch * seq
        self.rows = batch * (seq + ctx_len)

    def mod_spec(self, layer, which, tm, tile0=0):
        n_lat_tiles = self.rows_lat // tm
        tiles_per_seq = self.L // tm
        b = self.B

        def imap(t, *_):
            t = t + tile0
            row = jnp.where(t < n_lat_tiles, t // tiles_per_seq, b)
            return ((layer * MOD_ROWS + row) * 6 + which, 0, 0)

        return pl.BlockSpec((None, 1, self.D), imap)


def _mod_kernel(cond_ref, w_ref, b_ref, o_ref):
    s = _silu(cond_ref[...])
    w = w_ref[...]
    s_hi = s.astype(BF16)
    s_lo = (s - s_hi.astype(F32)).astype(BF16)
    w_hi = w.astype(BF16)
    w_lo = (w - w_hi.astype(F32)).astype(BF16)
    acc = _dot(s_hi, w_hi) + _dot(s_lo, w_hi) + _dot(s_hi, w_lo)
    o_ref[...] = acc + b_ref[...]


def _modulation_table(cond, ada_w, ada_b):
    depth, d, n = ada_w.shape
    tn = n // 4
    out = pl.pallas_call(
        _mod_kernel,
        grid=(depth, n // tn),
        in_specs=[
            pl.BlockSpec((MOD_ROWS, d), lambda l, j: (0, 0)),
            pl.BlockSpec((None, d, tn), lambda l, j: (l, 0, j)),
            pl.BlockSpec((None, 1, tn), lambda l, j: (l, 0, j)),
        ],
        out_specs=pl.BlockSpec((None, MOD_ROWS, tn), lambda l, j: (l, 0, j)),
        out_shape=jax.ShapeDtypeStruct((depth, MOD_ROWS, n), F32),
        compiler_params=_params(2),
        name="mod_table",
    )(cond, ada_w, ada_b.reshape(depth, 1, n))
    return out.reshape(depth * MOD_ROWS * 6, 1, d)


def _ffn_kernel(h_ref, sh_ref, sc_ref, gt_ref, w1_ref, w3_ref, w2_ref, g_ref, b_ref, o_ref,
                *, chunk):
    h = h_ref[...]
    u = (h * (1.0 + sc_ref[...]) + sh_ref[...]).astype(BF16)
    hidden = w1_ref.shape[1]
    acc = jnp.zeros(h.shape, F32)
    for c in range(0, hidden, chunk):
        a = _dot(u, w1_ref[:, c:c + chunk])
        b = _dot(u, w3_ref[:, c:c + chunk])
        g = (_silu(a) * b).astype(BF16)
        acc = acc + _dot(g, w2_ref[c:c + chunk, :])
    z = DEEPNORM_ALPHA * h + gt_ref[...] * acc
    o_ref[...] = _layer_norm(z, g_ref[...], b_ref[...])


def _ffn(geom, layer, h, mods, w1, w3, w2, ln_g, ln_b, n_rows, tm):
    d = geom.D
    hidden = w1.shape[1]
    row = pl.BlockSpec((tm, d), lambda t: (t, 0))
    return pl.pallas_call(
        functools.partial(_ffn_kernel, chunk=2 * LANES),
        grid=(n_rows // tm,),
        in_specs=[row, geom.mod_spec(layer, 3, tm), geom.mod_spec(layer, 4, tm),
                  geom.mod_spec(layer, 5, tm),
                  _const_spec((d, hidden)), _const_spec((d, hidden)), _const_spec((hidden, d)),
                  _const_spec((1, d)), _const_spec((1, d))],
        out_specs=row,
        out_shape=jax.ShapeDtypeStruct((n_rows, d), F32),
        compiler_params=_params(1),
        name="ffn",
    )(h, mods, mods, mods, w1, w3, w2, ln_g, ln_b)


def _proj_res_kernel(x_ref, w_ref, h_ref, gt_ref, g_ref, b_ref, o_ref):
    y = _dot(x_ref[...], w_ref[...])
    z = DEEPNORM_ALPHA * h_ref[...] + gt_ref[...] * y
    o_ref[...] = _layer_norm(z, g_ref[...], b_ref[...])


def _proj_res(geom, layer, x, w, h, mods, ln_g, ln_b, n_rows, tm):
    d = geom.D
    k = x.shape[1]
    return pl.pallas_call(
        _proj_res_kernel,
        grid=(n_rows // tm,),
        in_specs=[pl.BlockSpec((tm, k), lambda t: (t, 0)), _const_spec((k, d)),
                  pl.BlockSpec((tm, d), lambda t: (t, 0)), geom.mod_spec(layer, 2, tm),
                  _const_spec((1, d)), _const_spec((1, d))],
        out_specs=pl.BlockSpec((tm, d), lambda t: (t, 0)),
        out_shape=jax.ShapeDtypeStruct((n_rows, d), F32),
        compiler_params=_params(1),
        name="proj_res",
    )(x, w, h, mods, ln_g, ln_b)


def _seq_edges(t, tm, geom):
    n_lat_tiles = geom.rows_lat // tm
    lat_tps = geom.L // tm
    ctx_tps = max(geom.C // tm, 1)
    in_lat = t < n_lat_tiles
    pos = jnp.where(in_lat, t % lat_tps, (t - n_lat_tiles) % ctx_tps)
    tps = jnp.where(in_lat, lat_tps, ctx_tps)
    return pos == 0, pos == tps - 1


def _outside_rows(n_win_rows, halo, tm, is_start, is_end):
    row = lax.broadcasted_iota(jnp.int32, (n_win_rows, 1), 0)
    return jnp.logical_or(jnp.logical_and(is_start, row < halo),
                          jnp.logical_and(is_end, row >= halo + tm))


def _halo_specs(tm, halo, n_cols, n_rows_total, tile0=0):
    per = tm // halo
    last = n_rows_total // halo - 1
    prev = pl.BlockSpec((halo, n_cols), lambda t: (jnp.maximum((t + tile0) * per - 1, 0), 0))
    cur = pl.BlockSpec((tm, n_cols), lambda t: (t + tile0, 0))
    nxt = pl.BlockSpec((halo, n_cols), lambda t: (jnp.minimum((t + tile0 + 1) * per, last), 0))
    return prev, cur, nxt


def _depthwise(win_ref, w_ref, width, first, tm, n_cols, emit, w_col0=0, row_block=64):
    phases = {}
    for k in range(width):
        a, b = divmod(first + k, SUBLANES)
        phases.setdefault(b, []).append((a, k))
    for c in range(0, n_cols, LANES):
        for r in range(0, tm, row_block):
            acc = None
            for b, taps in sorted(phases.items()):
                z = None
                for a, k in taps:
                    lo = r + SUBLANES * a
                    term = (win_ref[lo:lo + row_block + SUBLANES, c:c + LANES]
                            * w_ref[k:k + 1, w_col0 + c:w_col0 + c + LANES])
                    z = term if z is None else z + term
                zb = z[b:b + row_block]
                acc = zb if acc is None else acc + zb
            emit(r, c, acc)


CONV_HALO = 16


def _conv_layer_kernel(hp_ref, hc_ref, hn_ref, sh_ref, sc_ref, gt_ref, wa_ref, wg_ref, ba_ref,
                       bg_ref, dww_ref, dwb_ref, ng_ref, nb_ref, w2_ref, b2_ref, g_ref, b_ref,
                       o_ref, win_ref, cv_ref, *, geom, tm):
    halo = CONV_HALO
    is_start, is_end = _seq_edges(pl.program_id(0), tm, geom)
    hwin = jnp.concatenate([hp_ref[...], hc_ref[...], hn_ref[...]], axis=0)
    u = (hwin * (1.0 + sc_ref[...]) + sh_ref[...]).astype(BF16)
    a = _dot(u, wa_ref[...]) + ba_ref[...]
    g = _dot(u, wg_ref[...]) + bg_ref[...]
    outside = _outside_rows(tm + 2 * halo, halo, tm, is_start, is_end)
    win_ref[...] = jnp.where(outside, 0.0, a * _sigmoid(g))

    def emit(r, c, acc):
        cv_ref[r:r + acc.shape[0], c:c + LANES] = acc + dwb_ref[:, c:c + LANES]

    _depthwise(win_ref, dww_ref, CONV_WIDTH, halo - (CONV_WIDTH - 1) // 2, tm, geom.D, emit)
    v = _silu(_layer_norm(cv_ref[...], ng_ref[...], nb_ref[...])).astype(BF16)
    y = _dot(v, w2_ref[...]) + b2_ref[...]
    z = DEEPNORM_ALPHA * hc_ref[...] + gt_ref[...] * y
    o_ref[...] = _layer_norm(z, g_ref[...], b_ref[...])


def _conv_layer(geom, layer, h, mods, w1, b1, dw_w, dw_b, n_g, n_b, w2, b2, ln_g, ln_b,
                n_rows, tm):
    d = geom.D
    prev, cur, nxt = _halo_specs(tm, CONV_HALO, d, h.shape[0])
    half = lambda j: pl.BlockSpec((d, d), lambda t: (0, j), pipeline_mode=pl.Buffered(1))
    bhalf = lambda j: pl.BlockSpec((1, d), lambda t: (0, j), pipeline_mode=pl.Buffered(1))
    vec = _const_spec((1, d))
    return pl.pallas_call(
        functools.partial(_conv_layer_kernel, geom=geom, tm=tm),
        grid=(n_rows // tm,),
        in_specs=[prev, cur, nxt, geom.mod_spec(layer, 0, tm), geom.mod_spec(layer, 1, tm),
                  geom.mod_spec(layer, 2, tm), half(0), half(1), bhalf(0), bhalf(1),
                  _const_spec((CONV_WIDTH, d)), vec, vec, vec, _const_spec((d, d)), vec, vec, vec],
        out_specs=pl.BlockSpec((tm, d), lambda t: (t, 0)),
        out_shape=jax.ShapeDtypeStruct((n_rows, d), F32),
        scratch_shapes=[pltpu.VMEM((tm + 2 * CONV_HALO, d), F32), pltpu.VMEM((tm, d), F32)],
        compiler_params=_params(1),
        name="conv_layer",
    )(h, h, h, mods, mods, mods, w1, w1, b1, b1, dw_w, dw_b, n_g, n_b, w2, b2, ln_g, ln_b)


SSM_HALO = 8


def _ssm_in_kernel(hp_ref, hc_ref, hn_ref, sh_ref, sc_ref, wz_ref, wx_ref, wdh_ref, wdl_ref,
                   dtb_ref, cw_ref, cb_ref, *rest, geom, tm, tile0, col_chunk):
    z_ref, x_ref, dt_ref, win_ref = rest[-4:]
    halo = SSM_HALO
    is_start, is_end = _seq_edges(pl.program_id(0) + tile0, tm, geom)
    hwin = jnp.concatenate([hp_ref[...], hc_ref[...], hn_ref[...]], axis=0)
    uf = hwin * (1.0 + sc_ref[...]) + sh_ref[...]
    u = uf.astype(BF16)
    u_lo = (uf - u.astype(F32)).astype(BF16)
    outside = _outside_rows(tm + 2 * halo, halo, tm, is_start, is_end)
    own = slice(halo, halo + tm)

    for c in range(0, wz_ref.shape[1], col_chunk):
        z_ref[:, c:c + col_chunk] = _dot(u, wz_ref[:, c:c + col_chunk])[own].astype(BF16)

    for i, c in enumerate(range(0, wx_ref.shape[1], col_chunk)):
        win = win_ref.at[i % 2]
        win[...] = jnp.where(outside, 0.0, _dot(u, wx_ref[:, c:c + col_chunk]))

        def emit(r, cc, acc, c=c):
            col = slice(c + cc, c + cc + LANES)
            x_ref[r:r + acc.shape[0], col] = _silu(acc + cb_ref[:, col]).astype(BF16)

        _depthwise(win, cw_ref, SSM_CONV_WIDTH, halo - (SSM_CONV_WIDTH - 1) // 2, tm, col_chunk,
                   emit, w_col0=c)

    d = (_dot(u, wdh_ref[...]) + _dot(u_lo, wdh_ref[...]) + _dot(u, wdl_ref[...]))[own]
    d = d + dtb_ref[...]
    dt_ref[...] = jnp.maximum(d, 0.0) + jnp.log1p(jnp.exp(-jnp.abs(d)))


def _ssm_in(geom, layer, h, mods, wz, wx, wd_hi, wd_lo, dt_bias, conv_w, conv_b, tm, row0,
            n_tiles, prev_out=None):
    d = geom.D
    nz, nx = wz.shape[1], wx.shape[1]
    n_rows = h.shape[0]
    tile0 = row0 // tm
    col_chunk = 8 * LANES
    prev, cur, nxt = _halo_specs(tm, SSM_HALO, d, n_rows, tile0)
    args = [h, h, h, mods, mods, wz, wx, wd_hi, wd_lo, dt_bias, conv_w, conv_b]
    in_specs = [prev, cur, nxt, geom.mod_spec(layer, 0, tm, tile0),
                geom.mod_spec(layer, 1, tm, tile0),
                _const_spec((d, nz)), _const_spec((d, nx)),
                _const_spec((d, LANES)), _const_spec((d, LANES)), _const_spec((1, LANES)),
                _const_spec((SSM_CONV_WIDTH, nx)), _const_spec((1, nx))]
    aliases = {}
    if prev_out is not None:
        for o in prev_out:
            aliases[len(args)] = len(aliases)
            args.append(o)
            in_specs.append(pl.BlockSpec(memory_space=pl.ANY))
    return pl.pallas_call(
        functools.partial(_ssm_in_kernel, geom=geom, tm=tm, tile0=tile0, col_chunk=col_chunk),
        grid=(n_tiles,),
        in_specs=in_specs,
        out_specs=[pl.BlockSpec((tm, nz), lambda t: (t + tile0, 0)),
                   pl.BlockSpec((tm, nx), lambda t: (t + tile0, 0)),
                   pl.BlockSpec((tm, LANES), lambda t: (t + tile0, 0))],
        out_shape=[jax.ShapeDtypeStruct((n_rows, nz), BF16),
                   jax.ShapeDtypeStruct((n_rows, nx), BF16),
                   jax.ShapeDtypeStruct((n_rows, LANES), F32)],
        input_output_aliases=aliases,
        scratch_shapes=[pltpu.VMEM((2, tm + 2 * SSM_HALO, col_chunk), F32)],
        compiler_params=_params(1),
        name="ssm_in",
    )(*args)


def _ssd_direction(x_ref, b_ref, c_ref, dt_ref, dtt_ref, alog_row_ref, alog_col_ref, st_ref,
                   y_ref, *, backward):
    q = SSM_CHUNK
    n_state = SSM_STATE
    hpg = 4
    ii = lax.broadcasted_iota(jnp.int32, (q, q), 0)
    jj = lax.broadcasted_iota(jnp.int32, (q, q), 1)
    mask = (ii <= jj) if backward else (ii >= jj)
    tri = mask.astype(BF16)
    tri_t = ((ii >= jj) if backward else (ii <= jj)).astype(BF16)

    dt = dt_ref[...]
    dtt = dtt_ref[...]
    a_col = dt * (-jnp.exp(alog_row_ref[...]))
    a_row = dtt * (-jnp.exp(alog_col_ref[...]))
    cum_col = sum(_dot(tri, p) for p in _split3(a_col))
    cum_row = sum(_dot(p, tri_t) for p in _split3(a_row))
    last = 0 if backward else q - 1
    e_col = jnp.exp(cum_col)
    cum_last_col = cum_row[:, last:last + 1]
    w_row = jnp.exp(cum_last_col - cum_row) * dtt
    e_last_row = jnp.exp(cum_col[last:last + 1, :])

    lane = lax.broadcasted_iota(jnp.int32, (q, LANES), 1)
    low_half = lane < SSM_HEAD_DIM
    base = 32 * (1 if backward else 0)

    for g in range(SSM_GROUPS):
        bg = b_ref[:, g * n_state:(g + 1) * n_state]
        cg = c_ref[:, g * n_state:(g + 1) * n_state]
        cb = _dot_nt(cg, bg)
        bt = bg.astype(F32).T
        cg32 = cg.astype(F32)
        for pair in range(hpg // 2):
            slot = g * (hpg // 2) + pair
            xp = x_ref[:, slot * LANES:(slot + 1) * LANES]
            hs = st_ref[slot]
            rhs = jnp.concatenate([xp, hs.astype(BF16)], axis=0)
            ys, ss, els = [], [], []
            for r in range(2):
                hl = base + g * hpg + pair * 2 + r
                cc = cum_col[:, hl:hl + 1]
                cr = cum_row[hl:hl + 1, :]
                seg = jnp.exp(jnp.where(mask, cc - cr, NEG_BIG))
                m = (cb * seg * dtt[hl:hl + 1, :]).astype(BF16)
                ce = (cg32 * e_col[:, hl:hl + 1]).astype(BF16)
                ys.append(_dot(jnp.concatenate([m, ce], axis=1), rhs))
                bts = (bt * w_row[hl:hl + 1, :]).astype(BF16)
                ss.append(_dot(bts, xp))
                els.append(e_last_row[:, hl:hl + 1])
            y_ref[:, slot * LANES:(slot + 1) * LANES] = jnp.where(low_half, ys[0], ys[1])
            low_n = low_half[:n_state]
            decay = jnp.where(low_n, els[0], els[1])
            st_ref[slot] = hs * decay + jnp.where(low_n, ss[0], ss[1])


def _ssd_kernel(xf_ref, bf_ref, cf_ref, dtf_ref, dttf_ref,
                xb_ref, bb_ref, cb_ref, dtb_ref, dttb_ref,
                alog_row_ref, alog_col_ref, yf_ref, yb_ref, stf_ref, stb_ref):
    @pl.when(pl.program_id(1) == 0)
    def _():
        stf_ref[...] = jnp.zeros(stf_ref.shape, F32)
        stb_ref[...] = jnp.zeros(stb_ref.shape, F32)

    _ssd_direction(xf_ref, bf_ref, cf_ref, dtf_ref, dttf_ref, alog_row_ref, alog_col_ref,
                   stf_ref, yf_ref, backward=False)
    _ssd_direction(xb_ref, bb_ref, cb_ref, dtb_ref, dttb_ref, alog_row_ref, alog_col_ref,
                   stb_ref, yb_ref, backward=True)


def _ssd(geom, xbc, dt, dtt, alog_row, alog_col, d_inner, with_ctx_out):
    q = SSM_CHUNK
    n_bc = SSM_GROUPS * SSM_STATE
    lat_chunks = geom.L // q
    ctx_chunks = geom.C // q
    lat_total = geom.rows_lat // q
    steps = lat_chunks + ctx_chunks

    def fwd_chunk(b, s):
        return jnp.where(s < ctx_chunks, lat_total + b * ctx_chunks + s,
                         b * lat_chunks + (s - ctx_chunks))

    def bwd_chunk(b, s):
        return jnp.where(s < ctx_chunks, lat_total + b * ctx_chunks + (ctx_chunks - 1 - s),
                         b * lat_chunks + (lat_chunks - 1 - (s - ctx_chunks)))

    def specs(chunk):
        return [pl.BlockSpec((q, d_inner), lambda b, s: (chunk(b, s), 0)),
                pl.BlockSpec((q, n_bc), lambda b, s: (chunk(b, s), d_inner // n_bc)),
                pl.BlockSpec((q, n_bc), lambda b, s: (chunk(b, s), d_inner // n_bc + 1)),
                pl.BlockSpec((q, LANES), lambda b, s: (chunk(b, s), 0)),
                pl.BlockSpec((LANES, q), lambda b, s: (0, chunk(b, s)))]

    n_rows = geom.rows
    y_shape = jax.ShapeDtypeStruct((n_rows, d_inner), F32)
    state = pltpu.VMEM((SSM_GROUPS * 2, SSM_STATE, LANES), F32)
    return pl.pallas_call(
        _ssd_kernel,
        grid=(geom.B, steps),
        in_specs=specs(fwd_chunk) + specs(bwd_chunk)
        + [pl.BlockSpec((1, LANES), lambda b, s: (0, 0)),
           pl.BlockSpec((LANES, 1), lambda b, s: (0, 0))],
        out_specs=[pl.BlockSpec((q, d_inner), lambda b, s: (fwd_chunk(b, s), 0)),
                   pl.BlockSpec((q, d_inner), lambda b, s: (bwd_chunk(b, s), 0))],
        out_shape=[y_shape, y_shape],
        scratch_shapes=[state, state],
        compiler_params=_params(2, ("parallel", "arbitrary")),
        name="ssd_scan",
    )(xbc, xbc, xbc, dt, dtt, xbc, xbc, xbc, dt, dtt, alog_row, alog_col)


def _ssm_out_kernel(yf_ref, yb_ref, x_ref, z_ref, dsk_ref, ng_ref, w_ref, h_ref, gt_ref,
                    g_ref, b_ref, o_ref):
    d_inner = yf_ref.shape[1]
    gw = d_inner // SSM_GROUPS
    acc = jnp.zeros(o_ref.shape, F32)
    for c in range(0, d_inner, gw):
        sl = slice(c, c + gw)
        y = yf_ref[:, sl] + yb_ref[:, sl] + dsk_ref[:, sl] * x_ref[:, sl].astype(F32)
        gt = y * _silu(z_ref[:, sl].astype(F32))
        ms = jnp.mean(gt * gt, axis=-1, keepdims=True)
        gn = (gt * lax.rsqrt(ms + RMS_EPS) * ng_ref[:, sl]).astype(BF16)
        acc = acc + _dot(gn, w_ref[sl, :])
    zz = DEEPNORM_ALPHA * h_ref[...] + gt_ref[...] * acc
    o_ref[...] = _layer_norm(zz, g_ref[...], b_ref[...])


def _ssm_out(geom, layer, yf, yb, xbc, z, d_skip, norm_g, out_w, h, mods, ln_g, ln_b,
             n_rows, tm):
    d = geom.D
    d_inner = z.shape[1]
    wide = pl.BlockSpec((tm, d_inner), lambda t: (t, 0))
    row = pl.BlockSpec((tm, d), lambda t: (t, 0))
    return pl.pallas_call(
        _ssm_out_kernel,
        grid=(n_rows // tm,),
        in_specs=[wide, wide, wide, wide, _const_spec((1, d_inner)), _const_spec((1, d_inner)),
                  _const_spec((d_inner, d)), row, geom.mod_spec(layer, 2, tm),
                  _const_spec((1, d)), _const_spec((1, d))],
        out_specs=row,
        out_shape=jax.ShapeDtypeStruct((n_rows, d), F32),
        compiler_params=_params(1),
        name="ssm_out",
    )(yf, yb, xbc, z, d_skip, norm_g, out_w, h, mods, ln_g, ln_b)


def _qkv_kernel(h_ref, sh_ref, sc_ref, wq_ref, wk_ref, wv_ref, pq_ref, pk_ref, gq_ref, gk_ref,
                cos_ref, sin_ref, q_ref, k_ref, v_ref):
    u = (h_ref[...] * (1.0 + sc_ref[...]) + sh_ref[...]).astype(BF16)
    cos = cos_ref[...]
    sin = sin_ref[...]

    def norm_rope(x, p_ref, g_ref):
        n = x.shape[1]
        ms = _dot((x * x).astype(BF16), p_ref[...])
        xn = x * lax.rsqrt(ms + RMS_EPS) * g_ref[...]
        lane = lax.broadcasted_iota(jnp.int32, xn.shape, 1)
        partner = jnp.where(lane % 2 == 0, pltpu.roll(xn, n - 1, axis=1),
                            pltpu.roll(xn, 1, axis=1))
        reps = n // cos.shape[1]
        return xn * jnp.tile(cos, (1, reps)) + partner * jnp.tile(sin, (1, reps))

    q = norm_rope(_dot(u, wq_ref[...]), pq_ref, gq_ref)
    q_ref[...] = (q * (ATTN_HEAD_DIM ** -0.5 * math.log2(math.e))).astype(BF16)
    k_ref[...] = norm_rope(_dot(u, wk_ref[...]), pk_ref, gk_ref).astype(BF16)
    v = _dot(u, wv_ref[...])
    lane = lax.broadcasted_iota(jnp.int32, v.shape, 1)
    v_ref[...] = jnp.where(lane % LANES < ATTN_HEAD_DIM, v, 1.0).astype(BF16)


def _qkv(geom, layer, h, mods, wq, wk2, wv2, pq, pk, gq, gk, cos_t, sin_t, n_rows, tm):
    d = geom.D
    nq, nk = wq.shape[1], wk2.shape[1]
    n_lat_tiles = geom.rows_lat // tm
    tps = geom.L // tm
    table = pl.BlockSpec((tm, LANES), lambda t: (jnp.where(t < n_lat_tiles, t % tps, tps), 0))
    return pl.pallas_call(
        _qkv_kernel,
        grid=(n_rows // tm,),
        in_specs=[pl.BlockSpec((tm, d), lambda t: (t, 0)),
                  geom.mod_spec(layer, 0, tm), geom.mod_spec(layer, 1, tm),
                  _const_spec((d, nq)), _const_spec((d, nk)), _const_spec((d, nk)),
                  _const_spec((nq, nq)), _const_spec((nk, nk)),
                  _const_spec((1, nq)), _const_spec((1, nk)), table, table],
        out_specs=[pl.BlockSpec((tm, nq), lambda t: (t, 0)),
                   pl.BlockSpec((tm, nk), lambda t: (t, 0)),
                   pl.BlockSpec((tm, nk), lambda t: (t, 0))],
        out_shape=[jax.ShapeDtypeStruct((n_rows, nq), BF16),
                   jax.ShapeDtypeStruct((n_rows, nk), BF16),
                   jax.ShapeDtypeStruct((n_rows, nk), BF16)],
        compiler_params=_params(1),
        name="attn_qkv",
    )(h, mods, mods, wq, wk2, wv2, pq, pk, gq, gk, cos_t, sin_t)


def _flash_kernel(q_ref, kl_ref, vl_ref, kc_ref, vc_ref, *rest, n_lat_chunks, ck):
    o_ref, s_ref, m_ref, acc_ref = rest[-4:]
    tq = q_ref.shape[0]
    lane = lax.broadcasted_iota(jnp.int32, (tq, LANES), 1)
    low_half = lane < ATTN_HEAD_DIM
    n_heads = 4

    qms = []
    for pair in range(n_heads // 2):
        q2 = q_ref[:, pair * LANES:(pair + 1) * LANES].astype(F32)
        qms.append(jnp.where(low_half, q2, 0.0).astype(BF16))
        qms.append(jnp.where(low_half, 0.0, q2).astype(BF16))

    def update(ss, vc):
        for h, s in enumerate(ss):
            m = m_ref[h]
            m_new = jnp.maximum(m, jnp.max(s, axis=-1, keepdims=True))
            p = jnp.exp2(s - m_new).astype(BF16)
            acc_ref[h] = jnp.exp2(m - m_new) * acc_ref[h] + _dot(p, vc)
            m_ref[h] = m_new

    def put_scores(slot, kc):
        for h, qm in enumerate(qms):
            s_ref[slot, h] = _dot_nt(qm, kc)

    def lat_rows(chunk):
        return pl.ds(pl.multiple_of(chunk * ck, ck), ck)

    m_ref[...] = jnp.full(m_ref.shape, NEG_BIG, F32)
    acc_ref[...] = jnp.zeros(acc_ref.shape, F32)
    update([_dot_nt(qm, kc_ref[...]) for qm in qms], vc_ref[...])
    if n_lat_chunks:
        put_scores(0, kl_ref[lat_rows(0), :])

        def body(t, _):
            c0 = 2 * t
            put_scores(1, kl_ref[lat_rows(c0 + 1), :])
            update([s_ref[0, h] for h in range(n_heads)], vl_ref[lat_rows(c0), :])
            put_scores(0, kl_ref[lat_rows(jnp.minimum(c0 + 2, n_lat_chunks - 1)), :])
            update([s_ref[1, h] for h in range(n_heads)], vl_ref[lat_rows(c0 + 1), :])
            return 0

        lax.fori_loop(0, n_lat_chunks // 2, body, 0)

    accs = [acc_ref[h] for h in range(n_heads)]
    outs = [acc * (1.0 / pltpu.roll(acc, ATTN_HEAD_DIM, axis=1)) for acc in accs]
    for pair in range(n_heads // 2):
        o_pair = jnp.where(low_half, outs[2 * pair],
                           pltpu.roll(outs[2 * pair + 1], ATTN_HEAD_DIM, axis=1))
        o_ref[:, pair * LANES:(pair + 1) * LANES] = o_pair.astype(BF16)


def _flash(geom, q, k2, v2, tq, ck, latent_queries, o_prev=None):
    b, l, c = geom.B, geom.L, geom.C
    gw = 4 * ATTN_HEAD_DIM
    ctx_block0 = geom.rows_lat // c
    if latent_queries:
        nq = l // tq
        q_spec = pl.BlockSpec((tq, gw), lambda bi, g, i: (bi * nq + i, g))
        kl_rows, n_lat_chunks = l, l // ck
        kl_map = lambda bi, g, i: (bi, g)
    else:
        nq = 1
        q_spec = pl.BlockSpec((tq, gw), lambda bi, g, i: (ctx_block0 + bi, g))
        kl_rows, n_lat_chunks = ck, 0
        kl_map = lambda bi, g, i: (0, g)
    kl_spec = pl.BlockSpec((kl_rows, LANES), kl_map)
    kc_spec = pl.BlockSpec((c, LANES), lambda bi, g, i: (ctx_block0 + bi, g))
    args = [q, k2, v2, k2, v2]
    in_specs = [q_spec, kl_spec, kl_spec, kc_spec, kc_spec]
    aliases = {}
    if o_prev is not None:
        args.append(o_prev)
        in_specs.append(pl.BlockSpec(memory_space=pl.ANY))
        aliases = {len(args) - 1: 0}
    return pl.pallas_call(
        functools.partial(_flash_kernel, n_lat_chunks=n_lat_chunks, ck=ck),
        grid=(b, ATTN_KV_HEADS, nq),
        in_specs=in_specs,
        out_specs=q_spec,
        out_shape=jax.ShapeDtypeStruct(q.shape, BF16),
        input_output_aliases=aliases,
        scratch_shapes=[
            pltpu.VMEM((2, 4, tq, ck) if n_lat_chunks else (2, 4, 8, LANES), F32),
            pltpu.VMEM((4, tq, 1), F32), pltpu.VMEM((4, tq, LANES), F32)],
        compiler_params=_params(3),
        name="flash_lat" if latent_queries else "flash_ctx",
    )(*args)


def _rope_tables(seq, tm):
    rows = seq // GRID_W
    row = jnp.repeat(jnp.arange(rows, dtype=F32), GRID_W)
    col = jnp.tile(jnp.arange(GRID_W, dtype=F32), rows)
    axis_dim = ATTN_HEAD_DIM // 2
    inv_freq = ROPE_THETA ** (-jnp.arange(0, axis_dim, 2, dtype=F32) / axis_dim)
    ang = jnp.concatenate([row[:, None] * inv_freq, col[:, None] * inv_freq], axis=-1)
    cos = jnp.repeat(jnp.cos(ang), 2, axis=-1)
    sign = jnp.tile(jnp.array([-1.0, 1.0], F32), ATTN_HEAD_DIM // 2)
    sin = jnp.repeat(jnp.sin(ang), 2, axis=-1) * sign
    reps = LANES // ATTN_HEAD_DIM
    cos = jnp.concatenate([jnp.tile(cos, (1, reps)), jnp.ones((tm, LANES), F32)], axis=0)
    sin = jnp.concatenate([jnp.tile(sin, (1, reps)), jnp.zeros((tm, LANES), F32)], axis=0)
    return cos, sin


def _head_mean_matrix(n, head_dim):
    idx = jnp.arange(n) // head_dim
    return jnp.where(idx[:, None] == idx[None, :], 1.0 / head_dim, 0.0).astype(BF16)


def kernel(x, c, ctx, c_ctx, ada_w, ada_b, ln_mix_g, ln_mix_b, ln_ffn_g, ln_ffn_b, ffn_w1, ffn_w3, ffn_w2, conv_pw1_w, conv_pw1_b, conv_dw_w, conv_dw_b, conv_norm_g, conv_norm_b, conv_pw2_w, conv_pw2_b, ssm_in_w, ssm_conv_w, ssm_conv_b, ssm_a_log, ssm_dt_bias, ssm_d, ssm_norm_g, ssm_out_w, attn_qkv_w, attn_q_norm, attn_k_norm, attn_out_w):
    batch, seq, d = x.shape
    ctx_len = ctx.shape[1]
    geom = _Geom(batch, seq, ctx_len, d)
    tm = math.gcd(512, math.gcd(seq, batch * ctx_len))
    tmc = math.gcd(256, math.gcd(seq, ctx_len))
    assert batch < MOD_ROWS and seq % GRID_W == 0 and ctx_len % SSM_CHUNK == 0

    cond = jnp.zeros((MOD_ROWS, d), F32).at[:batch].set(c).at[batch].set(c_ctx)
    mods = _modulation_table(cond, ada_w, ada_b)

    h = jnp.concatenate([x.reshape(batch * seq, d), ctx.reshape(batch * ctx_len, d)], axis=0)
    vec = lambda a: a.reshape(1, -1)

    for i in range(DEPTH):
        last = i == DEPTH - 1
        kind, j = i % N_MIXERS, i // N_MIXERS
        n_rows = geom.rows_lat if (last and kind == 0) else h.shape[0]
        n_rows = min(n_rows, h.shape[0])
        g_mix, b_mix = vec(ln_mix_g[i]), vec(ln_mix_b[i])

        if kind == 0:
            h = _conv_layer(geom, i, h, mods, conv_pw1_w[j].astype(BF16), vec(conv_pw1_b[j]),
                            conv_dw_w[j], vec(conv_dw_b[j]), vec(conv_norm_g[j]),
                            vec(conv_norm_b[j]), conv_pw2_w[j].astype(BF16), vec(conv_pw2_b[j]),
                            g_mix, b_mix, n_rows, tmc)
        elif kind == 1:
            in_w = ssm_in_w[j]
            d_inner = ssm_norm_g.shape[1]
            n_xbc = ssm_conv_w.shape[2]
            n_heads2 = ssm_a_log.shape[1]
            wz = in_w[:, :d_inner].astype(BF16)
            wx = in_w[:, d_inner:d_inner + n_xbc].astype(BF16)
            wd = jnp.pad(in_w[:, d_inner + n_xbc:], ((0, 0), (0, LANES - n_heads2)))
            wd_hi = wd.astype(BF16)
            wd_lo = (wd - wd_hi.astype(F32)).astype(BF16)
            dt_bias = jnp.pad(ssm_dt_bias[j], (0, LANES - n_heads2)).reshape(1, LANES)
            ssm_args = (geom, i, h, mods, wz, wx, wd_hi, wd_lo, dt_bias, ssm_conv_w[j],
                        vec(ssm_conv_b[j]))
            outs = _ssm_in(*ssm_args, tm, 0, geom.rows_lat // tm)
            z, xbc, dt = _ssm_in(*ssm_args, tmc, geom.rows_lat, (n_rows - geom.rows_lat) // tmc,
                                 prev_out=outs)
            a_log = jnp.pad(ssm_a_log[j], (0, LANES - n_heads2), constant_values=NEG_BIG)
            yf, yb = _ssd(geom, xbc, dt, dt.T, a_log.reshape(1, LANES), a_log.reshape(LANES, 1),
                          d_inner, True)
            d_skip = jnp.repeat(ssm_d[j], SSM_HEAD_DIM).reshape(1, d_inner)
            h = _ssm_out(geom, i, yf, yb, xbc, z, d_skip, vec(ssm_norm_g[j]),
                         ssm_out_w[j].astype(BF16), h, mods, g_mix, b_mix, n_rows, tm)
        else:
            w = attn_qkv_w[j]
            hd = ATTN_HEAD_DIM
            n_q = attn_out_w.shape[1]
            n_kv = (w.shape[1] - n_q) // 2
            wq = w[:, :n_q].astype(BF16)
            wk = w[:, n_q:n_q + n_kv].reshape(d, ATTN_KV_HEADS, 1, hd)
            wk2 = jnp.broadcast_to(wk, (d, ATTN_KV_HEADS, 2, hd)).reshape(d, 2 * n_kv)
            wv = w[:, n_q + n_kv:].reshape(d, ATTN_KV_HEADS, 1, hd)
            wv2 = jnp.concatenate([wv, jnp.zeros_like(wv)], axis=2).reshape(d, 2 * n_kv)
            gq = jnp.tile(attn_q_norm[j], n_q // hd).reshape(1, n_q)
            gk = jnp.tile(attn_k_norm[j], 2 * n_kv // hd).reshape(1, 2 * n_kv)
            cos_t, sin_t = _rope_tables(seq, tm)
            q, k2, v2 = _qkv(geom, i, h, mods, wq, wk2.astype(BF16), wv2.astype(BF16),
                             _head_mean_matrix(n_q, hd), _head_mean_matrix(2 * n_kv, hd),
                             gq, gk, cos_t, sin_t, n_rows, tm)
            tq = math.gcd(512, seq)
            ck = math.gcd(512, seq // 2)
            o = _flash(geom, q, k2, v2, tq, ck, True)
            o = _flash(geom, q, k2, v2, ctx_len, ck, False, o_prev=o)
            h = _proj_res(geom, i, o, attn_out_w[j].astype(BF16), h, mods, g_mix, b_mix,
                          n_rows, tm)

        h = _ffn(geom, i, h, mods, ffn_w1[i].astype(BF16), ffn_w3[i].astype(BF16),
                 ffn_w2[i].astype(BF16), vec(ln_ffn_g[i]), vec(ln_ffn_b[i]), n_rows, tm)

    return h[:geom.rows_lat].reshape(batch, seq, d)
```

```python
import functools
import math

import jax
import jax.numpy as jnp
from jax import lax
from jax.experimental import pallas as pl
from jax.experimental.pallas import tpu as pltpu

F32 = jnp.float32
BF16 = jnp.bfloat16

DEPTH = 4
GRID_W = 64
N_MIXERS = 3
DEEPNORM_ALPHA = (2 * DEPTH) ** 0.25
LN_EPS = 1e-5
RMS_EPS = 1e-6
CONV_WIDTH = 31
SSM_HEAD_DIM = 64
SSM_GROUPS = 8
SSM_STATE = 128
SSM_CONV_WIDTH = 5
SSM_CHUNK = 128
ATTN_HEAD_DIM = 64
ATTN_KV_HEADS = 4
ROPE_THETA = 10000.0

LANES = 128
SUBLANES = 8
BF16_ROWS = 16
VMEM_LIMIT = 56 * 1024 * 1024

MOD_ROWS = 8
NEG_BIG = -1e30


def _dot(a, b):
    return jnp.dot(a, b, preferred_element_type=F32)


def _dot_nt(a, b):
    return lax.dot_general(a, b, (((1,), (1,)), ((), ())), preferred_element_type=F32)


def _sigmoid(x):
    return 1.0 / (1.0 + jnp.exp(-x))


def _silu(x):
    return x * _sigmoid(x)


def _layer_norm(z, g, b):
    mu = jnp.mean(z, axis=-1, keepdims=True)
    d = z - mu
    var = jnp.mean(d * d, axis=-1, keepdims=True)
    return d * lax.rsqrt(var + LN_EPS) * g + b


def _split3(a):
    a1 = a.astype(BF16)
    r1 = a - a1.astype(F32)
    a2 = r1.astype(BF16)
    a3 = (r1 - a2.astype(F32)).astype(BF16)
    return a1, a2, a3


def _const_spec(shape):
    nd = len(shape)
    return pl.BlockSpec(shape, lambda *_: (0,) * nd, pipeline_mode=pl.Buffered(1))


def _params(n_grid_axes, sem=None):
    return pltpu.CompilerParams(
        dimension_semantics=sem or ("parallel",) * n_grid_axes,
        vmem_limit_bytes=VMEM_LIMIT)


class _Geom:
    def __init__(self, batch, seq, ctx_len, d_model):
        self.B, self.L, self.C, self.D = batch, seq, ctx_len, d_model
        self.rows_lat = batch * seq
        self.rows = batch * (seq + ctx_len)

    def mod_spec(self, layer, which, tm, tile0=0):
        n_lat_tiles = self.rows_lat // tm
        tiles_per_seq = self.L // tm
        b = self.B

        def imap(t, *_):
            t = t + tile0
            row = jnp.where(t < n_lat_tiles, t // tiles_per_seq, b)
            return ((layer * MOD_ROWS + row) * 6 + which, 0, 0)

        return pl.BlockSpec((None, 1, self.D), imap)


def _mod_kernel(cond_ref, w_ref, b_ref, o_ref):
    s = _silu(cond_ref[...])
    w = w_ref[...]
    s_hi = s.astype(BF16)
    s_lo = (s - s_hi.astype(F32)).astype(BF16)
    w_hi = w.astype(BF16)
    w_lo = (w - w_hi.astype(F32)).astype(BF16)
    acc = _dot(s_hi, w_hi) + _dot(s_lo, w_hi) + _dot(s_hi, w_lo)
    o_ref[...] = acc + b_ref[...]


def _modulation_table(cond, ada_w, ada_b):
    depth, d, n = ada_w.shape
    tn = n // 4
    out = pl.pallas_call(
        _mod_kernel,
        grid=(depth, n // tn),
        in_specs=[
            pl.BlockSpec((MOD_ROWS, d), lambda l, j: (0, 0)),
            pl.BlockSpec((None, d, tn), lambda l, j: (l, 0, j)),
            pl.BlockSpec((None, 1, tn), lambda l, j: (l, 0, j)),
        ],
        out_specs=pl.BlockSpec((None, MOD_ROWS, tn), lambda l, j: (l, 0, j)),
        out_shape=jax.ShapeDtypeStruct((depth, MOD_ROWS, n), F32),
        compiler_params=_params(2),
        name="mod_table",
    )(cond, ada_w, ada_b.reshape(depth, 1, n))
    return out.reshape(depth * MOD_ROWS * 6, 1, d)


def _ffn_kernel(h_ref, sh_ref, sc_ref, gt_ref, w1_ref, w3_ref, w2_ref, g_ref, b_ref, o_ref,
                *, chunk):
    h = h_ref[...]
    u = (h * (1.0 + sc_ref[...]) + sh_ref[...]).astype(BF16)
    hidden = w1_ref.shape[1]
    acc = jnp.zeros(h.shape, F32)
    for c in range(0, hidden, chunk):
        a = _dot(u, w1_ref[:, c:c + chunk])
        b = _dot(u, w3_ref[:, c:c + chunk])
        g = (_silu(a) * b).astype(BF16)
        acc = acc + _dot(g, w2_ref[c:c + chunk, :])
    z = DEEPNORM_ALPHA * h + gt_ref[...] * acc
    o_ref[...] = _layer_norm(z, g_ref[...], b_ref[...])


def _ffn(geom, layer, h, mods, w1, w3, w2, ln_g, ln_b, n_rows, tm):
    d = geom.D
    hidden = w1.shape[1]
    row = pl.BlockSpec((tm, d), lambda t: (t, 0))
    return pl.pallas_call(
        functools.partial(_ffn_kernel, chunk=2 * LANES),
        grid=(n_rows // tm,),
        in_specs=[row, geom.mod_spec(layer, 3, tm), geom.mod_spec(layer, 4, tm),
                  geom.mod_spec(layer, 5, tm),
                  _const_spec((d, hidden)), _const_spec((d, hidden)), _const_spec((hidden, d)),
                  _const_spec((1, d)), _const_spec((1, d))],
        out_specs=row,
        out_shape=jax.ShapeDtypeStruct((n_rows, d), F32),
        compiler_params=_params(1),
        name="ffn",
    )(h, mods, mods, mods, w1, w3, w2, ln_g, ln_b)


def _proj_res_kernel(x_ref, w_ref, h_ref, gt_ref, g_ref, b_ref, o_ref):
    y = _dot(x_ref[...], w_ref[...])
    z = DEEPNORM_ALPHA * h_ref[...] + gt_ref[...] * y
    o_ref[...] = _layer_norm(z, g_ref[...], b_ref[...])


def _proj_res(geom, layer, x, w, h, mods, ln_g, ln_b, n_rows, tm):
    d = geom.D
    k = x.shape[1]
    return pl.pallas_call(
        _proj_res_kernel,
        grid=(n_rows // tm,),
        in_specs=[pl.BlockSpec((tm, k), lambda t: (t, 0)), _const_spec((k, d)),
                  pl.BlockSpec((tm, d), lambda t: (t, 0)), geom.mod_spec(layer, 2, tm),
                  _const_spec((1, d)), _const_spec((1, d))],
        out_specs=pl.BlockSpec((tm, d), lambda t: (t, 0)),
        out_shape=jax.ShapeDtypeStruct((n_rows, d), F32),
        compiler_params=_params(1),
        name="proj_res",
    )(x, w, h, mods, ln_g, ln_b)


def _seq_edges(t, tm, geom):
    n_lat_tiles = geom.rows_lat // tm
    lat_tps = geom.L // tm
    ctx_tps = max(geom.C // tm, 1)
    in_lat = t < n_lat_tiles
    pos = jnp.where(in_lat, t % lat_tps, (t - n_lat_tiles) % ctx_tps)
    tps = jnp.where(in_lat, lat_tps, ctx_tps)
    return pos == 0, pos == tps - 1


def _outside_rows(n_win_rows, halo, tm, is_start, is_end):
    row = lax.broadcasted_iota(jnp.int32, (n_win_rows, 1), 0)
    return jnp.logical_or(jnp.logical_and(is_start, row < halo),
                          jnp.logical_and(is_end, row >= halo + tm))


def _halo_specs(tm, halo, n_cols, n_rows_total, tile0=0):
    per = tm // halo
    last = n_rows_total // halo - 1
    prev = pl.BlockSpec((halo, n_cols), lambda t: (jnp.maximum((t + tile0) * per - 1, 0), 0))
    cur = pl.BlockSpec((tm, n_cols), lambda t: (t + tile0, 0))
    nxt = pl.BlockSpec((halo, n_cols), lambda t: (jnp.minimum((t + tile0 + 1) * per, last), 0))
    return prev, cur, nxt


def _depthwise(win_ref, w_ref, width, first, tm, n_cols, emit, w_col0=0, row_block=64):
    phases = {}
    for k in range(width):
        a, b = divmod(first + k, SUBLANES)
        phases.setdefault(b, []).append((a, k))
    for c in range(0, n_cols, LANES):
        for r in range(0, tm, row_block):
            acc = None
            for b, taps in sorted(phases.items()):
                z = None
                for a, k in taps:
                    lo = r + SUBLANES * a
                    term = (win_ref[lo:lo + row_block + SUBLANES, c:c + LANES]
                            * w_ref[k:k + 1, w_col0 + c:w_col0 + c + LANES])
                    z = term if z is None else z + term
                zb = z[b:b + row_block]
                acc = zb if acc is None else acc + zb
            emit(r, c, acc)


CONV_HALO = 16


def _conv_layer_kernel(hp_ref, hc_ref, hn_ref, sh_ref, sc_ref, gt_ref, wa_ref, wg_ref, ba_ref,
                       bg_ref, dww_ref, dwb_ref, ng_ref, nb_ref, w2_ref, b2_ref, g_ref, b_ref,
                       o_ref, win_ref, cv_ref, *, geom, tm):
    halo = CONV_HALO
    is_start, is_end = _seq_edges(pl.program_id(0), tm, geom)
    hwin = jnp.concatenate([hp_ref[...], hc_ref[...], hn_ref[...]], axis=0)
    u = (hwin * (1.0 + sc_ref[...]) + sh_ref[...]).astype(BF16)
    a = _dot(u, wa_ref[...]) + ba_ref[...]
    g = _dot(u, wg_ref[...]) + bg_ref[...]
    outside = _outside_rows(tm + 2 * halo, halo, tm, is_start, is_end)
    win_ref[...] = jnp.where(outside, 0.0, a * _sigmoid(g))

    def emit(r, c, acc):
        cv_ref[r:r + acc.shape[0], c:c + LANES] = acc + dwb_ref[:, c:c + LANES]

    _depthwise(win_ref, dww_ref, CONV_WIDTH, halo - (CONV_WIDTH - 1) // 2, tm, geom.D, emit)
    v = _silu(_layer_norm(cv_ref[...], ng_ref[...], nb_ref[...])).astype(BF16)
    y = _dot(v, w2_ref[...]) + b2_ref[...]
    z = DEEPNORM_ALPHA * hc_ref[...] + gt_ref[...] * y
    o_ref[...] = _layer_norm(z, g_ref[...], b_ref[...])


def _conv_layer(geom, layer, h, mods, w1, b1, dw_w, dw_b, n_g, n_b, w2, b2, ln_g, ln_b,
                n_rows, tm):
    d = geom.D
    prev, cur, nxt = _halo_specs(tm, CONV_HALO, d, h.shape[0])
    half = lambda j: pl.BlockSpec((d, d), lambda t: (0, j), pipeline_mode=pl.Buffered(1))
    bhalf = lambda j: pl.BlockSpec((1, d), lambda t: (0, j), pipeline_mode=pl.Buffered(1))
    vec = _const_spec((1, d))
    return pl.pallas_call(
        functools.partial(_conv_layer_kernel, geom=geom, tm=tm),
        grid=(n_rows // tm,),
        in_specs=[prev, cur, nxt, geom.mod_spec(layer, 0, tm), geom.mod_spec(layer, 1, tm),
                  geom.mod_spec(layer, 2, tm), half(0), half(1), bhalf(0), bhalf(1),
                  _const_spec((CONV_WIDTH, d)), vec, vec, vec, _const_spec((d, d)), vec, vec, vec],
        out_specs=pl.BlockSpec((tm, d), lambda t: (t, 0)),
        out_shape=jax.ShapeDtypeStruct((n_rows, d), F32),
        scratch_shapes=[pltpu.VMEM((tm + 2 * CONV_HALO, d), F32), pltpu.VMEM((tm, d), F32)],
        compiler_params=_params(1),
        name="conv_layer",
    )(h, h, h, mods, mods, mods, w1, w1, b1, b1, dw_w, dw_b, n_g, n_b, w2, b2, ln_g, ln_b)


SSM_HALO = 8


def _ssm_in_kernel(hp_ref, hc_ref, hn_ref, sh_ref, sc_ref, wz_ref, wx_ref, wdh_ref, wdl_ref,
                   dtb_ref, cw_ref, cb_ref, *rest, geom, tm, tile0, col_chunk):
    z_ref, x_ref, dt_ref, win_a, win_b = rest[-5:]
    wins = (win_a, win_b)
    halo = SSM_HALO
    is_start, is_end = _seq_edges(pl.program_id(0) + tile0, tm, geom)
    hwin = jnp.concatenate([hp_ref[...], hc_ref[...], hn_ref[...]], axis=0)
    uf = hwin * (1.0 + sc_ref[...]) + sh_ref[...]
    u = uf.astype(BF16)
    u_lo = (uf - u.astype(F32)).astype(BF16)
    outside = _outside_rows(tm + 2 * halo, halo, tm, is_start, is_end)
    own = slice(halo, halo + tm)

    def project_x(i):
        c = i * col_chunk
        wins[i % 2][...] = jnp.where(outside, 0.0, _dot(u, wx_ref[:, c:c + col_chunk]))

    def conv_x(i):
        c = i * col_chunk

        def emit(r, cc, acc):
            col = slice(c + cc, c + cc + LANES)
            x_ref[r:r + acc.shape[0], col] = _silu(acc + cb_ref[:, col]).astype(BF16)

        _depthwise(wins[i % 2], cw_ref, SSM_CONV_WIDTH,
                   halo - (SSM_CONV_WIDTH - 1) // 2, tm, col_chunk, emit, w_col0=c)

    def project_z(i):
        c = i * col_chunk
        z_ref[:, c:c + col_chunk] = _dot(u, wz_ref[:, c:c + col_chunk])[own].astype(BF16)

    def project_dt():
        d = (_dot(u, wdh_ref[...]) + _dot(u_lo, wdh_ref[...]) + _dot(u, wdl_ref[...]))[own]
        d = d + dtb_ref[...]
        dt_ref[...] = jnp.maximum(d, 0.0) + jnp.log1p(jnp.exp(-jnp.abs(d)))

    n_x = wx_ref.shape[1] // col_chunk
    matmuls = ([functools.partial(project_x, i) for i in range(1, n_x)]
               + [functools.partial(project_z, i) for i in range(wz_ref.shape[1] // col_chunk)]
               + [project_dt])
    project_x(0)
    for i in range(n_x):
        matmuls.pop(0)()
        conv_x(i)
    for f in matmuls:
        f()


def _ssm_in(geom, layer, h, mods, wz, wx, wd_hi, wd_lo, dt_bias, conv_w, conv_b, tm, row0,
            n_tiles, prev_out=None):
    d = geom.D
    nz, nx = wz.shape[1], wx.shape[1]
    n_rows = h.shape[0]
    tile0 = row0 // tm
    col_chunk = 8 * LANES
    prev, cur, nxt = _halo_specs(tm, SSM_HALO, d, n_rows, tile0)
    args = [h, h, h, mods, mods, wz, wx, wd_hi, wd_lo, dt_bias, conv_w, conv_b]
    in_specs = [prev, cur, nxt, geom.mod_spec(layer, 0, tm, tile0),
                geom.mod_spec(layer, 1, tm, tile0),
                _const_spec((d, nz)), _const_spec((d, nx)),
                _const_spec((d, LANES)), _const_spec((d, LANES)), _const_spec((1, LANES)),
                _const_spec((SSM_CONV_WIDTH, nx)), _const_spec((1, nx))]
    aliases = {}
    if prev_out is not None:
        for o in prev_out:
            aliases[len(args)] = len(aliases)
            args.append(o)
            in_specs.append(pl.BlockSpec(memory_space=pl.ANY))
    return pl.pallas_call(
        functools.partial(_ssm_in_kernel, geom=geom, tm=tm, tile0=tile0, col_chunk=col_chunk),
        grid=(n_tiles,),
        in_specs=in_specs,
        out_specs=[pl.BlockSpec((tm, nz), lambda t: (t + tile0, 0)),
                   pl.BlockSpec((tm, nx), lambda t: (t + tile0, 0)),
                   pl.BlockSpec((tm, LANES), lambda t: (t + tile0, 0))],
        out_shape=[jax.ShapeDtypeStruct((n_rows, nz), BF16),
                   jax.ShapeDtypeStruct((n_rows, nx), BF16),
                   jax.ShapeDtypeStruct((n_rows, LANES), F32)],
        input_output_aliases=aliases,
        scratch_shapes=[pltpu.VMEM((tm + 2 * SSM_HALO, col_chunk), F32)] * 2,
        compiler_params=_params(1),
        name="ssm_in",
    )(*args)


def _ssd_direction(x_ref, b_ref, c_ref, dt_ref, dtt_ref, alog_row_ref, alog_col_ref, st_ref,
                   y_ref, *, backward):
    q = SSM_CHUNK
    n_state = SSM_STATE
    hpg = 4
    ii = lax.broadcasted_iota(jnp.int32, (q, q), 0)
    jj = lax.broadcasted_iota(jnp.int32, (q, q), 1)
    mask = (ii <= jj) if backward else (ii >= jj)
    tri = mask.astype(BF16)
    tri_t = ((ii >= jj) if backward else (ii <= jj)).astype(BF16)

    dt = dt_ref[...]
    dtt = dtt_ref[...]
    a_col = dt * (-jnp.exp(alog_row_ref[...]))
    a_row = dtt * (-jnp.exp(alog_col_ref[...]))
    cum_col = sum(_dot(tri, p) for p in _split3(a_col))
    cum_row = sum(_dot(p, tri_t) for p in _split3(a_row))
    last = 0 if backward else q - 1
    cum_last_col = cum_row[:, last:last + 1]
    w_row = jnp.exp(cum_last_col - cum_row) * dtt
    e_last_row = jnp.exp(cum_col[last:last + 1, :])

    lane = lax.broadcasted_iota(jnp.int32, (q, LANES), 1)
    low_half = lane < SSM_HEAD_DIM
    base = 32 * (1 if backward else 0)

    def group(g):
        bg = b_ref[:, g * n_state:(g + 1) * n_state]
        cg = c_ref[:, g * n_state:(g + 1) * n_state]
        cb = _dot_nt(cg, bg)
        bt = bg.astype(F32).T
        cg32 = cg.astype(F32)
        for pair in range(hpg // 2):
            slot = g * (hpg // 2) + pair
            xp = x_ref[:, slot * LANES:(slot + 1) * LANES]
            hs = st_ref[slot]
            rhs = jnp.concatenate([xp, hs.astype(BF16)], axis=0)
            ys, ss, els = [], [], []
            for r in range(2):
                hl = base + g * hpg + pair * 2 + r
                cc = jnp.broadcast_to(cum_col[:, hl:hl + 1], (q, LANES))
                cr = cum_row[hl:hl + 1, :]
                seg = jnp.exp(jnp.where(mask, cc - cr, NEG_BIG))
                m = (cb * seg * dtt[hl:hl + 1, :]).astype(BF16)
                ce = (cg32 * jnp.exp(cc)).astype(BF16)
                ys.append(_dot(jnp.concatenate([m, ce], axis=1), rhs))
                bts = (bt * w_row[hl:hl + 1, :]).astype(BF16)
                ss.append(_dot(bts, xp))
                els.append(e_last_row[:, hl:hl + 1])
            y_ref[:, slot * LANES:(slot + 1) * LANES] = jnp.where(
                low_half, ys[0], ys[1]).astype(y_ref.dtype)
            low_n = low_half[:n_state]
            decay = jnp.where(low_n, els[0], els[1])
            st_ref[slot] = hs * decay + jnp.where(low_n, ss[0], ss[1])

    return group


def _ssd_kernel(xf_ref, bf_ref, cf_ref, dtf_ref, dttf_ref,
                xb_ref, bb_ref, cb_ref, dtb_ref, dttb_ref,
                alog_row_ref, alog_col_ref, yf_ref, yb_ref, stf_ref, stb_ref):
    @pl.when(pl.program_id(1) == 0)
    def _():
        stf_ref[...] = jnp.zeros(stf_ref.shape, F32)
        stb_ref[...] = jnp.zeros(stb_ref.shape, F32)

    fwd = _ssd_direction(xf_ref, bf_ref, cf_ref, dtf_ref, dttf_ref, alog_row_ref, alog_col_ref,
                         stf_ref, yf_ref, backward=False)
    bwd = _ssd_direction(xb_ref, bb_ref, cb_ref, dtb_ref, dttb_ref, alog_row_ref, alog_col_ref,
                         stb_ref, yb_ref, backward=True)
    for g in range(SSM_GROUPS):
        fwd(g)
        bwd(g)


def _ssd(geom, xbc, dt, dtt, alog_row, alog_col, d_inner, with_ctx_out):
    q = SSM_CHUNK
    n_bc = SSM_GROUPS * SSM_STATE
    lat_chunks = geom.L // q
    ctx_chunks = geom.C // q
    lat_total = geom.rows_lat // q
    steps = lat_chunks + ctx_chunks

    def fwd_chunk(b, s):
        return jnp.where(s < ctx_chunks, lat_total + b * ctx_chunks + s,
                         b * lat_chunks + (s - ctx_chunks))

    def bwd_chunk(b, s):
        return jnp.where(s < ctx_chunks, lat_total + b * ctx_chunks + (ctx_chunks - 1 - s),
                         b * lat_chunks + (lat_chunks - 1 - (s - ctx_chunks)))

    def specs(chunk):
        return [pl.BlockSpec((q, d_inner), lambda b, s: (chunk(b, s), 0)),
                pl.BlockSpec((q, n_bc), lambda b, s: (chunk(b, s), d_inner // n_bc)),
                pl.BlockSpec((q, n_bc), lambda b, s: (chunk(b, s), d_inner // n_bc + 1)),
                pl.BlockSpec((q, LANES), lambda b, s: (chunk(b, s), 0)),
                pl.BlockSpec((LANES, q), lambda b, s: (0, chunk(b, s)))]

    n_rows = geom.rows
    y_shape = jax.ShapeDtypeStruct((n_rows, d_inner), BF16)
    state = pltpu.VMEM((SSM_GROUPS * 2, SSM_STATE, LANES), F32)
    return pl.pallas_call(
        _ssd_kernel,
        grid=(geom.B, steps),
        in_specs=specs(fwd_chunk) + specs(bwd_chunk)
        + [pl.BlockSpec((1, LANES), lambda b, s: (0, 0)),
           pl.BlockSpec((LANES, 1), lambda b, s: (0, 0))],
        out_specs=[pl.BlockSpec((q, d_inner), lambda b, s: (fwd_chunk(b, s), 0)),
                   pl.BlockSpec((q, d_inner), lambda b, s: (bwd_chunk(b, s), 0))],
        out_shape=[y_shape, y_shape],
        scratch_shapes=[state, state],
        compiler_params=_params(2, ("parallel", "arbitrary")),
        name="ssd_scan",
    )(xbc, xbc, xbc, dt, dtt, xbc, xbc, xbc, dt, dtt, alog_row, alog_col)


def _ssm_out_kernel(yf_ref, yb_ref, x_ref, z_ref, dsk_ref, ng_ref, w_ref, h_ref, gt_ref,
                    g_ref, b_ref, o_ref):
    d_inner = yf_ref.shape[1]
    gw = d_inner // SSM_GROUPS
    acc = jnp.zeros(o_ref.shape, F32)
    for c in range(0, d_inner, gw):
        sl = slice(c, c + gw)
        y = (yf_ref[:, sl].astype(F32) + yb_ref[:, sl].astype(F32)
             + dsk_ref[:, sl] * x_ref[:, sl].astype(F32))
        gt = y * _silu(z_ref[:, sl].astype(F32))
        ms = jnp.mean(gt * gt, axis=-1, keepdims=True)
        gn = (gt * lax.rsqrt(ms + RMS_EPS) * ng_ref[:, sl]).astype(BF16)
        acc = acc + _dot(gn, w_ref[sl, :])
    zz = DEEPNORM_ALPHA * h_ref[...] + gt_ref[...] * acc
    o_ref[...] = _layer_norm(zz, g_ref[...], b_ref[...])


def _ssm_out(geom, layer, yf, yb, xbc, z, d_skip, norm_g, out_w, h, mods, ln_g, ln_b,
             n_rows, tm):
    d = geom.D
    d_inner = z.shape[1]
    wide = pl.BlockSpec((tm, d_inner), lambda t: (t, 0))
    row = pl.BlockSpec((tm, d), lambda t: (t, 0))
    return pl.pallas_call(
        _ssm_out_kernel,
        grid=(n_rows // tm,),
        in_specs=[wide, wide, wide, wide, _const_spec((1, d_inner)), _const_spec((1, d_inner)),
                  _const_spec((d_inner, d)), row, geom.mod_spec(layer, 2, tm),
                  _const_spec((1, d)), _const_spec((1, d))],
        out_specs=row,
        out_shape=jax.ShapeDtypeStruct((n_rows, d), F32),
        compiler_params=_params(1),
        name="ssm_out",
    )(yf, yb, xbc, z, d_skip, norm_g, out_w, h, mods, ln_g, ln_b)


def _qkv_kernel(h_ref, sh_ref, sc_ref, wq_ref, wk_ref, wv_ref, pq_ref, pk_ref, gq_ref, gk_ref,
                cos_ref, sin_ref, q_ref, k_ref, v_ref):
    u = (h_ref[...] * (1.0 + sc_ref[...]) + sh_ref[...]).astype(BF16)
    cos = cos_ref[...]
    sin = sin_ref[...]

    def norm_rope(x, p_ref, g_ref):
        n = x.shape[1]
        ms = _dot((x * x).astype(BF16), p_ref[...])
        xn = x * lax.rsqrt(ms + RMS_EPS) * g_ref[...]
        lane = lax.broadcasted_iota(jnp.int32, xn.shape, 1)
        partner = jnp.where(lane % 2 == 0, pltpu.roll(xn, n - 1, axis=1),
                            pltpu.roll(xn, 1, axis=1))
        reps = n // cos.shape[1]
        return xn * jnp.tile(cos, (1, reps)) + partner * jnp.tile(sin, (1, reps))

    q = norm_rope(_dot(u, wq_ref[...]), pq_ref, gq_ref)
    q_ref[...] = (q * (ATTN_HEAD_DIM ** -0.5 * math.log2(math.e))).astype(BF16)
    k_ref[...] = norm_rope(_dot(u, wk_ref[...]), pk_ref, gk_ref).astype(BF16)
    v_ref[...] = _dot(u, wv_ref[...]).astype(BF16)


def _qkv(geom, layer, h, mods, wq, wk2, wv, pq, pk, gq, gk, cos_t, sin_t, n_rows, tm):
    d = geom.D
    nq, nk, nv = wq.shape[1], wk2.shape[1], wv.shape[1]
    n_lat_tiles = geom.rows_lat // tm
    tps = geom.L // tm
    table = pl.BlockSpec((tm, LANES), lambda t: (jnp.where(t < n_lat_tiles, t % tps, tps), 0))
    return pl.pallas_call(
        _qkv_kernel,
        grid=(n_rows // tm,),
        in_specs=[pl.BlockSpec((tm, d), lambda t: (t, 0)),
                  geom.mod_spec(layer, 0, tm), geom.mod_spec(layer, 1, tm),
                  _const_spec((d, nq)), _const_spec((d, nk)), _const_spec((d, nv)),
                  _const_spec((nq, nq)), _const_spec((nk, nk)),
                  _const_spec((1, nq)), _const_spec((1, nk)), table, table],
        out_specs=[pl.BlockSpec((tm, nq), lambda t: (t, 0)),
                   pl.BlockSpec((tm, nk), lambda t: (t, 0)),
                   pl.BlockSpec((tm, nv), lambda t: (t, 0))],
        out_shape=[jax.ShapeDtypeStruct((n_rows, nq), BF16),
                   jax.ShapeDtypeStruct((n_rows, nk), BF16),
                   jax.ShapeDtypeStruct((n_rows, nv), BF16)],
        compiler_params=_params(1),
        name="attn_qkv",
    )(h, mods, mods, wq, wk2, wv, pq, pk, gq, gk, cos_t, sin_t)


def _flash_kernel(q_ref, kl_ref, vl_ref, kc_ref, vc_ref, *rest, n_lat_chunks, ck):
    n_heads = 4
    n_scratch = 4 * n_heads
    o_ref = rest[-n_scratch - 1]
    s_refs = (rest[-n_scratch:-3 * n_heads], rest[-3 * n_heads:-2 * n_heads])
    m_refs = rest[-2 * n_heads:-n_heads]
    acc_refs = rest[-n_heads:]
    tq = q_ref.shape[0]
    lane = lax.broadcasted_iota(jnp.int32, (tq, LANES), 1)
    low_half = lane < ATTN_HEAD_DIM

    qms = []
    for pair in range(n_heads // 2):
        q2 = q_ref[:, pair * LANES:(pair + 1) * LANES].astype(F32)
        qms.append(jnp.where(low_half, q2, 0.0).astype(BF16))
        qms.append(jnp.where(low_half, 0.0, q2).astype(BF16))

    def update_head(h, s, vc):
        m = m_refs[h][...]
        row_max = jnp.max(s, axis=-1, keepdims=True)
        m_new = jnp.maximum(m, jnp.broadcast_to(row_max, m.shape))
        m_wide = jnp.concatenate([m_new] * (s.shape[1] // LANES), axis=1)
        p = jnp.exp2(s - m_wide).astype(BF16)
        acc_refs[h][...] = jnp.exp2(m - m_new) * acc_refs[h][...] + _dot(p, vc)
        m_refs[h][...] = m_new

    def lat_rows(chunk):
        return pl.ds(pl.multiple_of(chunk * ck, ck), ck)

    def step(cur, nxt, next_chunk, cur_chunk):
        kc = kl_ref[lat_rows(next_chunk), :]
        vc = vl_ref[lat_rows(cur_chunk), :]
        for h in range(n_heads):
            s_refs[nxt][h][...] = _dot_nt(qms[h], kc)
            update_head(h, s_refs[cur][h][...], vc)

    for h in range(n_heads):
        m_refs[h][...] = jnp.full((tq, LANES), NEG_BIG, F32)
        acc_refs[h][...] = jnp.zeros((tq, LANES), F32)
    if n_lat_chunks:
        for h in range(n_heads):
            s_refs[0][h][...] = _dot_nt(qms[h], kl_ref[lat_rows(0), :])
    for h in range(n_heads):
        update_head(h, _dot_nt(qms[h], kc_ref[...]), vc_ref[...])
    if n_lat_chunks:
        def body(t, _):
            c0 = 2 * t
            step(0, 1, c0 + 1, c0)
            step(1, 0, jnp.minimum(c0 + 2, n_lat_chunks - 1), c0 + 1)
            return 0

        lax.fori_loop(0, n_lat_chunks // 2, body, 0)

    accs = [r[...] for r in acc_refs]
    outs = [acc * (1.0 / pltpu.roll(acc, ATTN_HEAD_DIM, axis=1)) for acc in accs]
    for pair in range(n_heads // 2):
        o_pair = jnp.where(low_half, outs[2 * pair],
                           pltpu.roll(outs[2 * pair + 1], ATTN_HEAD_DIM, axis=1))
        o_ref[:, pair * LANES:(pair + 1) * LANES] = o_pair.astype(BF16)


def _flash(geom, q, k2, v2, tq, ck, latent_queries, o_prev=None):
    b, l, c = geom.B, geom.L, geom.C
    gw = 4 * ATTN_HEAD_DIM
    ctx_block0 = geom.rows_lat // c
    if latent_queries:
        nq = l // tq
        q_spec = pl.BlockSpec((tq, gw), lambda bi, g, i: (bi * nq + i, g))
        kl_rows, n_lat_chunks = l, l // ck
        kl_map = lambda bi, g, i: (bi, g)
    else:
        nq = 1
        q_spec = pl.BlockSpec((tq, gw), lambda bi, g, i: (ctx_block0 + bi, g))
        kl_rows, n_lat_chunks = ck, 0
        kl_map = lambda bi, g, i: (0, g)
    kl_spec = pl.BlockSpec((kl_rows, LANES), kl_map)
    kc_spec = pl.BlockSpec((c, LANES), lambda bi, g, i: (ctx_block0 + bi, g))
    args = [q, k2, v2, k2, v2]
    in_specs = [q_spec, kl_spec, kl_spec, kc_spec, kc_spec]
    aliases = {}
    if o_prev is not None:
        args.append(o_prev)
        in_specs.append(pl.BlockSpec(memory_space=pl.ANY))
        aliases = {len(args) - 1: 0}
    return pl.pallas_call(
        functools.partial(_flash_kernel, n_lat_chunks=n_lat_chunks, ck=ck),
        grid=(b, ATTN_KV_HEADS, nq),
        in_specs=in_specs,
        out_specs=q_spec,
        out_shape=jax.ShapeDtypeStruct(q.shape, BF16),
        input_output_aliases=aliases,
        scratch_shapes=(
            [pltpu.VMEM((tq, ck) if n_lat_chunks else (SUBLANES, LANES), F32)] * 8
            + [pltpu.VMEM((tq, LANES), F32)] * 8),
        compiler_params=_params(3),
        name="flash_lat" if latent_queries else "flash_ctx",
    )(*args)


VT_ROWS = ATTN_HEAD_DIM + BF16_ROWS


def _flash_t_kernel(q_ref, kl_ref, vtl_ref, kc_ref, vtc_ref, *rest, n_lat_chunks, ck):
    n_heads = 4
    n_scratch = 4 * n_heads
    o_ref = rest[-n_scratch - 1]
    s_refs = (rest[-n_scratch:-3 * n_heads], rest[-3 * n_heads:-2 * n_heads])
    m_refs = rest[-2 * n_heads:-n_heads]
    acc_refs = rest[-n_heads:]
    tq = q_ref.shape[0]
    row = lax.broadcasted_iota(jnp.int32, (LANES, tq), 0)
    low_half = row < ATTN_HEAD_DIM

    qts = []
    for pair in range(n_heads // 2):
        q2t = q_ref[:, pair * LANES:(pair + 1) * LANES].astype(F32).T
        qts.append(jnp.where(low_half, q2t, 0.0).astype(BF16))
        qts.append(jnp.where(low_half, 0.0, q2t).astype(BF16))

    def update_head(h, st, vt):
        m = m_refs[h][...]
        m_new = jnp.maximum(m, jnp.max(st, axis=0, keepdims=True))
        p = jnp.exp2(st - m_new).astype(BF16)
        acc_refs[h][...] = jnp.exp2(m - m_new) * acc_refs[h][...] + _dot(vt, p)
        m_refs[h][...] = m_new

    def lat_rows(chunk):
        return pl.ds(pl.multiple_of(chunk * ck, ck), ck)

    def step(cur, nxt, next_chunk, cur_chunk):
        kc = kl_ref[lat_rows(next_chunk), :]
        vt = vtl_ref[cur_chunk]
        for h in range(n_heads):
            s_refs[nxt][h][...] = _dot(kc, qts[h])
            update_head(h, s_refs[cur][h][...], vt)

    for h in range(n_heads):
        m_refs[h][...] = jnp.full((1, tq), NEG_BIG, F32)
        acc_refs[h][...] = jnp.zeros((VT_ROWS, tq), F32)
    if n_lat_chunks:
        for h in range(n_heads):
            s_refs[0][h][...] = _dot(kl_ref[lat_rows(0), :], qts[h])
    for h in range(n_heads):
        update_head(h, _dot(kc_ref[...], qts[h]), vtc_ref[...])
    if n_lat_chunks:
        def body(t, _):
            c0 = 2 * t
            step(0, 1, c0 + 1, c0)
            step(1, 0, jnp.minimum(c0 + 2, n_lat_chunks - 1), c0 + 1)
            return 0

        lax.fori_loop(0, n_lat_chunks // 2, body, 0)

    outs_t = []
    for h in range(n_heads):
        acc = acc_refs[h][...]
        outs_t.append(acc[:ATTN_HEAD_DIM] * (1.0 / acc[ATTN_HEAD_DIM:ATTN_HEAD_DIM + 1]))
    for pair in range(n_heads // 2):
        o_t = jnp.concatenate([outs_t[2 * pair], outs_t[2 * pair + 1]], axis=0)
        o_ref[:, pair * LANES:(pair + 1) * LANES] = o_t.T.astype(BF16)


def _value_rows(geom, v, ck):
    n_kv, hd = ATTN_KV_HEADS, ATTN_HEAD_DIM
    n_rows = v.shape[0]
    vt = v.T.reshape(n_kv, hd, n_rows)
    vt = jnp.concatenate([vt, jnp.ones((n_kv, BF16_ROWS, n_rows), v.dtype)], axis=1)
    lat = vt[:, :, :geom.rows_lat].reshape(n_kv, VT_ROWS, geom.B, geom.L // ck, ck)
    ctx = vt[:, :, geom.rows_lat:].reshape(n_kv, VT_ROWS, geom.B, geom.C)
    return lat.transpose(2, 0, 3, 1, 4), ctx.transpose(2, 0, 1, 3)


def _flash_t(geom, q, k2, vt_lat, vt_ctx, tq, ck, latent_queries, o_prev=None):
    b, l, c = geom.B, geom.L, geom.C
    gw = 4 * ATTN_HEAD_DIM
    ctx_block0 = geom.rows_lat // c
    chunks = l // ck
    if latent_queries:
        nq = l // tq
        q_spec = pl.BlockSpec((tq, gw), lambda bi, g, i: (bi * nq + i, g))
        kl_rows, n_lat_chunks = l, chunks
        kl_map = lambda bi, g, i: (bi, g)
        vtl_map = lambda bi, g, i: (bi, g, 0, 0, 0)
    else:
        nq = 1
        q_spec = pl.BlockSpec((tq, gw), lambda bi, g, i: (ctx_block0 + bi, g))
        kl_rows, n_lat_chunks = ck, 0
        kl_map = lambda bi, g, i: (0, g)
        vtl_map = lambda bi, g, i: (0, g, 0, 0, 0)
    kl_spec = pl.BlockSpec((kl_rows, LANES), kl_map)
    vtl_spec = pl.BlockSpec((None, None, chunks, VT_ROWS, ck), vtl_map)
    kc_spec = pl.BlockSpec((c, LANES), lambda bi, g, i: (ctx_block0 + bi, g))
    vtc_spec = pl.BlockSpec((None, None, VT_ROWS, c), lambda bi, g, i: (bi, g, 0, 0))
    args = [q, k2, vt_lat, k2, vt_ctx]
    in_specs = [q_spec, kl_spec, vtl_spec, kc_spec, vtc_spec]
    aliases = {}
    if o_prev is not None:
        args.append(o_prev)
        in_specs.append(pl.BlockSpec(memory_space=pl.ANY))
        aliases = {len(args) - 1: 0}
    return pl.pallas_call(
        functools.partial(_flash_t_kernel, n_lat_chunks=n_lat_chunks, ck=ck),
        grid=(b, ATTN_KV_HEADS, nq),
        in_specs=in_specs,
        out_specs=q_spec,
        out_shape=jax.ShapeDtypeStruct(q.shape, BF16),
        input_output_aliases=aliases,
        scratch_shapes=(
            [pltpu.VMEM((ck, tq) if n_lat_chunks else (SUBLANES, LANES), F32)] * 8
            + [pltpu.VMEM((1, tq), F32)] * 4 + [pltpu.VMEM((VT_ROWS, tq), F32)] * 4),
        compiler_params=_params(3),
        name="flash_lat" if latent_queries else "flash_ctx",
    )(*args)


def _rope_tables(seq, tm):
    rows = seq // GRID_W
    row = jnp.repeat(jnp.arange(rows, dtype=F32), GRID_W)
    col = jnp.tile(jnp.arange(GRID_W, dtype=F32), rows)
    axis_dim = ATTN_HEAD_DIM // 2
    inv_freq = ROPE_THETA ** (-jnp.arange(0, axis_dim, 2, dtype=F32) / axis_dim)
    ang = jnp.concatenate([row[:, None] * inv_freq, col[:, None] * inv_freq], axis=-1)
    cos = jnp.repeat(jnp.cos(ang), 2, axis=-1)
    sign = jnp.tile(jnp.array([-1.0, 1.0], F32), ATTN_HEAD_DIM // 2)
    sin = jnp.repeat(jnp.sin(ang), 2, axis=-1) * sign
    reps = LANES // ATTN_HEAD_DIM
    cos = jnp.concatenate([jnp.tile(cos, (1, reps)), jnp.ones((tm, LANES), F32)], axis=0)
    sin = jnp.concatenate([jnp.tile(sin, (1, reps)), jnp.zeros((tm, LANES), F32)], axis=0)
    return cos, sin


def _head_mean_matrix(n, head_dim):
    idx = jnp.arange(n) // head_dim
    return jnp.where(idx[:, None] == idx[None, :], 1.0 / head_dim, 0.0).astype(BF16)


def kernel(x, c, ctx, c_ctx, ada_w, ada_b, ln_mix_g, ln_mix_b, ln_ffn_g, ln_ffn_b, ffn_w1, ffn_w3, ffn_w2, conv_pw1_w, conv_pw1_b, conv_dw_w, conv_dw_b, conv_norm_g, conv_norm_b, conv_pw2_w, conv_pw2_b, ssm_in_w, ssm_conv_w, ssm_conv_b, ssm_a_log, ssm_dt_bias, ssm_d, ssm_norm_g, ssm_out_w, attn_qkv_w, attn_q_norm, attn_k_norm, attn_out_w):
    batch, seq, d = x.shape
    ctx_len = ctx.shape[1]
    geom = _Geom(batch, seq, ctx_len, d)
    tm = math.gcd(512, math.gcd(seq, batch * ctx_len))
    tmc = math.gcd(256, math.gcd(seq, ctx_len))
    assert batch < MOD_ROWS and seq % GRID_W == 0 and ctx_len % SSM_CHUNK == 0

    cond = jnp.zeros((MOD_ROWS, d), F32).at[:batch].set(c).at[batch].set(c_ctx)
    mods = _modulation_table(cond, ada_w, ada_b)

    h = jnp.concatenate([x.reshape(batch * seq, d), ctx.reshape(batch * ctx_len, d)], axis=0)
    vec = lambda a: a.reshape(1, -1)

    for i in range(DEPTH):
        last = i == DEPTH - 1
        kind, j = i % N_MIXERS, i // N_MIXERS
        n_rows = geom.rows_lat if (last and kind == 0) else h.shape[0]
        n_rows = min(n_rows, h.shape[0])
        g_mix, b_mix = vec(ln_mix_g[i]), vec(ln_mix_b[i])

        if kind == 0:
            h = _conv_layer(geom, i, h, mods, conv_pw1_w[j].astype(BF16), vec(conv_pw1_b[j]),
                            conv_dw_w[j], vec(conv_dw_b[j]), vec(conv_norm_g[j]),
                            vec(conv_norm_b[j]), conv_pw2_w[j].astype(BF16), vec(conv_pw2_b[j]),
                            g_mix, b_mix, n_rows, tmc)
        elif kind == 1:
            in_w = ssm_in_w[j]
            d_inner = ssm_norm_g.shape[1]
            n_xbc = ssm_conv_w.shape[2]
            n_heads2 = ssm_a_log.shape[1]
            wz = in_w[:, :d_inner].astype(BF16)
            wx = in_w[:, d_inner:d_inner + n_xbc].astype(BF16)
            wd = jnp.pad(in_w[:, d_inner + n_xbc:], ((0, 0), (0, LANES - n_heads2)))
            wd_hi = wd.astype(BF16)
            wd_lo = (wd - wd_hi.astype(F32)).astype(BF16)
            dt_bias = jnp.pad(ssm_dt_bias[j], (0, LANES - n_heads2)).reshape(1, LANES)
            ssm_args = (geom, i, h, mods, wz, wx, wd_hi, wd_lo, dt_bias, ssm_conv_w[j],
                        vec(ssm_conv_b[j]))
            outs = _ssm_in(*ssm_args, tm, 0, geom.rows_lat // tm)
            z, xbc, dt = _ssm_in(*ssm_args, tmc, geom.rows_lat, (n_rows - geom.rows_lat) // tmc,
                                 prev_out=outs)
            a_log = jnp.pad(ssm_a_log[j], (0, LANES - n_heads2), constant_values=NEG_BIG)
            yf, yb = _ssd(geom, xbc, dt, dt.T, a_log.reshape(1, LANES), a_log.reshape(LANES, 1),
                          d_inner, True)
            d_skip = jnp.repeat(ssm_d[j], SSM_HEAD_DIM).reshape(1, d_inner)
            h = _ssm_out(geom, i, yf, yb, xbc, z, d_skip, vec(ssm_norm_g[j]),
                         ssm_out_w[j].astype(BF16), h, mods, g_mix, b_mix, n_rows, tm)
        else:
            w = attn_qkv_w[j]
            hd = ATTN_HEAD_DIM
            n_q = attn_out_w.shape[1]
            n_kv = (w.shape[1] - n_q) // 2
            wq = w[:, :n_q].astype(BF16)
            wk = w[:, n_q:n_q + n_kv].reshape(d, ATTN_KV_HEADS, 1, hd)
            wk2 = jnp.broadcast_to(wk, (d, ATTN_KV_HEADS, 2, hd)).reshape(d, 2 * n_kv)
            wv = w[:, n_q + n_kv:].astype(BF16)
            gq = jnp.tile(attn_q_norm[j], n_q // hd).reshape(1, n_q)
            gk = jnp.tile(attn_k_norm[j], 2 * n_kv // hd).reshape(1, 2 * n_kv)
            cos_t, sin_t = _rope_tables(seq, tm)
            q, k2, v = _qkv(geom, i, h, mods, wq, wk2.astype(BF16), wv,
                            _head_mean_matrix(n_q, hd), _head_mean_matrix(2 * n_kv, hd),
                            gq, gk, cos_t, sin_t, n_rows, tm)
            tq = math.gcd(512, seq)
            ck = math.gcd(512, seq // 2)
            vt_lat, vt_ctx = _value_rows(geom, v, ck)
            o = _flash_t(geom, q, k2, vt_lat, vt_ctx, tq, ck, True)
            o = _flash_t(geom, q, k2, vt_lat, vt_ctx, ctx_len, ck, False, o_prev=o)
            h = _proj_res(geom, i, o, attn_out_w[j].astype(BF16), h, mods, g_mix, b_mix,
                          n_rows, tm)

        h = _ffn(geom, i, h, mods, ffn_w1[i].astype(BF16), ffn_w3[i].astype(BF16),
                 ffn_w2[i].astype(BF16), vec(ln_ffn_g[i]), vec(ln_ffn_b[i]), n_rows, tm)

    return h[:geom.rows_lat].reshape(batch, seq, d)
```

```python
import functools
import math

import jax
import jax.numpy as jnp
from jax import lax
from jax.experimental import pallas as pl
from jax.experimental.pallas import tpu as pltpu

F32 = jnp.float32
BF16 = jnp.bfloat16

DEPTH = 4
GRID_W = 64
N_MIXERS = 3
DEEPNORM_ALPHA = (2 * DEPTH) ** 0.25
LN_EPS = 1e-5
RMS_EPS = 1e-6
CONV_WIDTH = 31
SSM_HEAD_DIM = 64
SSM_GROUPS = 8
SSM_STATE = 128
SSM_CONV_WIDTH = 5
SSM_CHUNK = 128
ATTN_HEAD_DIM = 64
ATTN_KV_HEADS = 4
ROPE_THETA = 10000.0

LANES = 128
SUBLANES = 8
BF16_ROWS = 16
VMEM_LIMIT = 56 * 1024 * 1024

MOD_ROWS = 8
NEG_BIG = -1e30


def _dot(a, b):
    return jnp.dot(a, b, preferred_element_type=F32)


def _dot_nt(a, b):
    return lax.dot_general(a, b, (((1,), (1,)), ((), ())), preferred_element_type=F32)


def _sigmoid(x):
    return 1.0 / (1.0 + jnp.exp(-x))


def _silu(x):
    return x * _sigmoid(x)


def _layer_norm(z, g, b):
    mu = jnp.mean(z, axis=-1, keepdims=True)
    d = z - mu
    var = jnp.mean(d * d, axis=-1, keepdims=True)
    return d * lax.rsqrt(var + LN_EPS) * g + b


def _split3(a):
    a1 = a.astype(BF16)
    r1 = a - a1.astype(F32)
    a2 = r1.astype(BF16)
    a3 = (r1 - a2.astype(F32)).astype(BF16)
    return a1, a2, a3


def _const_spec(shape):
    nd = len(shape)
    return pl.BlockSpec(shape, lambda *_: (0,) * nd, pipeline_mode=pl.Buffered(1))


def _params(n_grid_axes, sem=None):
    return pltpu.CompilerParams(
        dimension_semantics=sem or ("parallel",) * n_grid_axes,
        vmem_limit_bytes=VMEM_LIMIT)


class _Geom:
    def __init__(self, batch, seq, ctx_len, d_model):
        self.B, self.L, self.C, self.D = batch, seq, ctx_len, d_model
        self.rows_lat = batch * seq
        self.rows = batch * (seq + ctx_len)

    def mod_spec(self, layer, which, tm, tile0=0):
        n_lat_tiles = self.rows_lat // tm
        tiles_per_seq = self.L // tm
        b = self.B

        def imap(t, *_):
            t = t + tile0
            row = jnp.where(t < n_lat_tiles, t // tiles_per_seq, b)
            return ((layer * MOD_ROWS + row) * 6 + which, 0, 0)

        return pl.BlockSpec((None, 1, self.D), imap)


def _mod_kernel(cond_ref, w_ref, b_ref, o_ref):
    s = _silu(cond_ref[...])
    w = w_ref[...]
    s_hi = s.astype(BF16)
    s_lo = (s - s_hi.astype(F32)).astype(BF16)
    w_hi = w.astype(BF16)
    w_lo = (w - w_hi.astype(F32)).astype(BF16)
    acc = _dot(s_hi, w_hi) + _dot(s_lo, w_hi) + _dot(s_hi, w_lo)
    o_ref[...] = acc + b_ref[...]


def _modulation_table(cond, ada_w, ada_b):
    depth, d, n = ada_w.shape
    tn = n // 4
    out = pl.pallas_call(
        _mod_kernel,
        grid=(depth, n // tn),
        in_specs=[
            pl.BlockSpec((MOD_ROWS, d), lambda l, j: (0, 0)),
            pl.BlockSpec((None, d, tn), lambda l, j: (l, 0, j)),
            pl.BlockSpec((None, 1, tn), lambda l, j: (l, 0, j)),
        ],
        out_specs=pl.BlockSpec((None, MOD_ROWS, tn), lambda l, j: (l, 0, j)),
        out_shape=jax.ShapeDtypeStruct((depth, MOD_ROWS, n), F32),
        compiler_params=_params(2),
        name="mod_table",
    )(cond, ada_w, ada_b.reshape(depth, 1, n))
    return out.reshape(depth * MOD_ROWS * 6, 1, d)


def _ffn_kernel(h_ref, sh_ref, sc_ref, gt_ref, w1_ref, w3_ref, w2_ref, g_ref, b_ref, o_ref,
                *, chunk):
    h = h_ref[...]
    u = (h * (1.0 + sc_ref[...]) + sh_ref[...]).astype(BF16)
    hidden = w1_ref.shape[1]
    acc = jnp.zeros(h.shape, F32)
    for c in range(0, hidden, chunk):
        a = _dot(u, w1_ref[:, c:c + chunk])
        b = _dot(u, w3_ref[:, c:c + chunk])
        g = (_silu(a) * b).astype(BF16)
        acc = acc + _dot(g, w2_ref[c:c + chunk, :])
    z = DEEPNORM_ALPHA * h + gt_ref[...] * acc
    o_ref[...] = _layer_norm(z, g_ref[...], b_ref[...])


def _ffn(geom, layer, h, mods, w1, w3, w2, ln_g, ln_b, n_rows, tm):
    d = geom.D
    hidden = w1.shape[1]
    row = pl.BlockSpec((tm, d), lambda t: (t, 0))
    return pl.pallas_call(
        functools.partial(_ffn_kernel, chunk=2 * LANES),
        grid=(n_rows // tm,),
        in_specs=[row, geom.mod_spec(layer, 3, tm), geom.mod_spec(layer, 4, tm),
                  geom.mod_spec(layer, 5, tm),
                  _const_spec((d, hidden)), _const_spec((d, hidden)), _const_spec((hidden, d)),
                  _const_spec((1, d)), _const_spec((1, d))],
        out_specs=row,
        out_shape=jax.ShapeDtypeStruct((n_rows, d), F32),
        compiler_params=_params(1),
        name="ffn",
    )(h, mods, mods, mods, w1, w3, w2, ln_g, ln_b)


def _proj_res_kernel(x_ref, w_ref, h_ref, gt_ref, g_ref, b_ref, o_ref):
    y = _dot(x_ref[...], w_ref[...])
    z = DEEPNORM_ALPHA * h_ref[...] + gt_ref[...] * y
    o_ref[...] = _layer_norm(z, g_ref[...], b_ref[...])


def _proj_res(geom, layer, x, w, h, mods, ln_g, ln_b, n_rows, tm):
    d = geom.D
    k = x.shape[1]
    return pl.pallas_call(
        _proj_res_kernel,
        grid=(n_rows // tm,),
        in_specs=[pl.BlockSpec((tm, k), lambda t: (t, 0)), _const_spec((k, d)),
                  pl.BlockSpec((tm, d), lambda t: (t, 0)), geom.mod_spec(layer, 2, tm),
                  _const_spec((1, d)), _const_spec((1, d))],
        out_specs=pl.BlockSpec((tm, d), lambda t: (t, 0)),
        out_shape=jax.ShapeDtypeStruct((n_rows, d), F32),
        compiler_params=_params(1),
        name="proj_res",
    )(x, w, h, mods, ln_g, ln_b)


def _seq_edges(t, tm, geom):
    n_lat_tiles = geom.rows_lat // tm
    lat_tps = geom.L // tm
    ctx_tps = max(geom.C // tm, 1)
    in_lat = t < n_lat_tiles
    pos = jnp.where(in_lat, t % lat_tps, (t - n_lat_tiles) % ctx_tps)
    tps = jnp.where(in_lat, lat_tps, ctx_tps)
    return pos == 0, pos == tps - 1


def _outside_rows(n_win_rows, halo, tm, is_start, is_end):
    row = lax.broadcasted_iota(jnp.int32, (n_win_rows, 1), 0)
    return jnp.logical_or(jnp.logical_and(is_start, row < halo),
                          jnp.logical_and(is_end, row >= halo + tm))


def _halo_specs(tm, halo, n_cols, n_rows_total, tile0=0):
    per = tm // halo
    last = n_rows_total // halo - 1
    prev = pl.BlockSpec((halo, n_cols), lambda t: (jnp.maximum((t + tile0) * per - 1, 0), 0))
    cur = pl.BlockSpec((tm, n_cols), lambda t: (t + tile0, 0))
    nxt = pl.BlockSpec((halo, n_cols), lambda t: (jnp.minimum((t + tile0 + 1) * per, last), 0))
    return prev, cur, nxt


def _depthwise(win_ref, w_ref, width, first, tm, n_cols, emit, w_col0=0, row_block=64):
    phases = {}
    for k in range(width):
        a, b = divmod(first + k, SUBLANES)
        phases.setdefault(b, []).append((a, k))
    for c in range(0, n_cols, LANES):
        for r in range(0, tm, row_block):
            acc = None
            for b, taps in sorted(phases.items()):
                z = None
                for a, k in taps:
                    lo = r + SUBLANES * a
                    term = (win_ref[lo:lo + row_block + SUBLANES, c:c + LANES]
                            * w_ref[k:k + 1, w_col0 + c:w_col0 + c + LANES])
                    z = term if z is None else z + term
                zb = z[b:b + row_block]
                acc = zb if acc is None else acc + zb
            emit(r, c, acc)


CONV_HALO = 16


def _conv_layer_kernel(hp_ref, hc_ref, hn_ref, sh_ref, sc_ref, gt_ref, wa_ref, wg_ref, ba_ref,
                       bg_ref, dww_ref, dwb_ref, ng_ref, nb_ref, w2_ref, b2_ref, g_ref, b_ref,
                       o_ref, win_ref, cv_ref, *, geom, tm):
    halo = CONV_HALO
    is_start, is_end = _seq_edges(pl.program_id(0), tm, geom)
    hwin = jnp.concatenate([hp_ref[...], hc_ref[...], hn_ref[...]], axis=0)
    u = (hwin * (1.0 + sc_ref[...]) + sh_ref[...]).astype(BF16)
    a = _dot(u, wa_ref[...]) + ba_ref[...]
    g = _dot(u, wg_ref[...]) + bg_ref[...]
    outside = _outside_rows(tm + 2 * halo, halo, tm, is_start, is_end)
    win_ref[...] = jnp.where(outside, 0.0, a * _sigmoid(g))

    def emit(r, c, acc):
        cv_ref[r:r + acc.shape[0], c:c + LANES] = acc + dwb_ref[:, c:c + LANES]

    _depthwise(win_ref, dww_ref, CONV_WIDTH, halo - (CONV_WIDTH - 1) // 2, tm, geom.D, emit)
    v = _silu(_layer_norm(cv_ref[...], ng_ref[...], nb_ref[...])).astype(BF16)
    y = _dot(v, w2_ref[...]) + b2_ref[...]
    z = DEEPNORM_ALPHA * hc_ref[...] + gt_ref[...] * y
    o_ref[...] = _layer_norm(z, g_ref[...], b_ref[...])


def _conv_layer(geom, layer, h, mods, w1, b1, dw_w, dw_b, n_g, n_b, w2, b2, ln_g, ln_b,
                n_rows, tm):
    d = geom.D
    prev, cur, nxt = _halo_specs(tm, CONV_HALO, d, h.shape[0])
    half = lambda j: pl.BlockSpec((d, d), lambda t: (0, j), pipeline_mode=pl.Buffered(1))
    bhalf = lambda j: pl.BlockSpec((1, d), lambda t: (0, j), pipeline_mode=pl.Buffered(1))
    vec = _const_spec((1, d))
    return pl.pallas_call(
        functools.partial(_conv_layer_kernel, geom=geom, tm=tm),
        grid=(n_rows // tm,),
        in_specs=[prev, cur, nxt, geom.mod_spec(layer, 0, tm), geom.mod_spec(layer, 1, tm),
                  geom.mod_spec(layer, 2, tm), half(0), half(1), bhalf(0), bhalf(1),
                  _const_spec((CONV_WIDTH, d)), vec, vec, vec, _const_spec((d, d)), vec, vec, vec],
        out_specs=pl.BlockSpec((tm, d), lambda t: (t, 0)),
        out_shape=jax.ShapeDtypeStruct((n_rows, d), F32),
        scratch_shapes=[pltpu.VMEM((tm + 2 * CONV_HALO, d), F32), pltpu.VMEM((tm, d), F32)],
        compiler_params=_params(1),
        name="conv_layer",
    )(h, h, h, mods, mods, mods, w1, w1, b1, b1, dw_w, dw_b, n_g, n_b, w2, b2, ln_g, ln_b)


SSM_HALO = 8


def _ssm_in_kernel(hp_ref, hc_ref, hn_ref, sh_ref, sc_ref, wz_ref, wx_ref, wdh_ref, wdl_ref,
                   dtb_ref, cw_ref, cb_ref, *rest, geom, tm, tile0, col_chunk):
    z_ref, x_ref, dt_ref, win_a, win_b = rest[-5:]
    wins = (win_a, win_b)
    halo = SSM_HALO
    is_start, is_end = _seq_edges(pl.program_id(0) + tile0, tm, geom)
    hwin = jnp.concatenate([hp_ref[...], hc_ref[...], hn_ref[...]], axis=0)
    uf = hwin * (1.0 + sc_ref[...]) + sh_ref[...]
    u = uf.astype(BF16)
    u_lo = (uf - u.astype(F32)).astype(BF16)
    outside = _outside_rows(tm + 2 * halo, halo, tm, is_start, is_end)
    own = slice(halo, halo + tm)

    def project_x(i):
        c = i * col_chunk
        wins[i % 2][...] = jnp.where(outside, 0.0, _dot(u, wx_ref[:, c:c + col_chunk]))

    def conv_x(i):
        c = i * col_chunk

        def emit(r, cc, acc):
            col = slice(c + cc, c + cc + LANES)
            x_ref[r:r + acc.shape[0], col] = _silu(acc + cb_ref[:, col]).astype(BF16)

        _depthwise(wins[i % 2], cw_ref, SSM_CONV_WIDTH,
                   halo - (SSM_CONV_WIDTH - 1) // 2, tm, col_chunk, emit, w_col0=c)

    def project_z(i):
        c = i * col_chunk
        z_ref[:, c:c + col_chunk] = _dot(u, wz_ref[:, c:c + col_chunk])[own].astype(BF16)

    def project_dt():
        d = (_dot(u, wdh_ref[...]) + _dot(u_lo, wdh_ref[...]) + _dot(u, wdl_ref[...]))[own]
        d = d + dtb_ref[...]
        dt_ref[...] = jnp.maximum(d, 0.0) + jnp.log1p(jnp.exp(-jnp.abs(d)))

    n_x = wx_ref.shape[1] // col_chunk
    matmuls = ([functools.partial(project_x, i) for i in range(1, n_x)]
               + [functools.partial(project_z, i) for i in range(wz_ref.shape[1] // col_chunk)]
               + [project_dt])
    project_x(0)
    for i in range(n_x):
        matmuls.pop(0)()
        conv_x(i)
    for f in matmuls:
        f()


def _ssm_in(geom, layer, h, mods, wz, wx, wd_hi, wd_lo, dt_bias, conv_w, conv_b, tm, row0,
            n_tiles, prev_out=None):
    d = geom.D
    nz, nx = wz.shape[1], wx.shape[1]
    n_rows = h.shape[0]
    tile0 = row0 // tm
    col_chunk = 8 * LANES
    prev, cur, nxt = _halo_specs(tm, SSM_HALO, d, n_rows, tile0)
    args = [h, h, h, mods, mods, wz, wx, wd_hi, wd_lo, dt_bias, conv_w, conv_b]
    in_specs = [prev, cur, nxt, geom.mod_spec(layer, 0, tm, tile0),
                geom.mod_spec(layer, 1, tm, tile0),
                _const_spec((d, nz)), _const_spec((d, nx)),
                _const_spec((d, LANES)), _const_spec((d, LANES)), _const_spec((1, LANES)),
                _const_spec((SSM_CONV_WIDTH, nx)), _const_spec((1, nx))]
    aliases = {}
    if prev_out is not None:
        for o in prev_out:
            aliases[len(args)] = len(aliases)
            args.append(o)
            in_specs.append(pl.BlockSpec(memory_space=pl.ANY))
    return pl.pallas_call(
        functools.partial(_ssm_in_kernel, geom=geom, tm=tm, tile0=tile0, col_chunk=col_chunk),
        grid=(n_tiles,),
        in_specs=in_specs,
        out_specs=[pl.BlockSpec((tm, nz), lambda t: (t + tile0, 0)),
                   pl.BlockSpec((tm, nx), lambda t: (t + tile0, 0)),
                   pl.BlockSpec((tm, LANES), lambda t: (t + tile0, 0))],
        out_shape=[jax.ShapeDtypeStruct((n_rows, nz), BF16),
                   jax.ShapeDtypeStruct((n_rows, nx), BF16),
                   jax.ShapeDtypeStruct((n_rows, LANES), F32)],
        input_output_aliases=aliases,
        scratch_shapes=[pltpu.VMEM((tm + 2 * SSM_HALO, col_chunk), F32)] * 2,
        compiler_params=_params(1),
        name="ssm_in",
    )(*args)


def _ssd_direction(x_ref, b_ref, c_ref, dt_ref, dtt_ref, alog_row_ref, alog_col_ref, st_ref,
                   y_ref, *, backward):
    q = SSM_CHUNK
    n_state = SSM_STATE
    hpg = 4
    ii = lax.broadcasted_iota(jnp.int32, (q, q), 0)
    jj = lax.broadcasted_iota(jnp.int32, (q, q), 1)
    mask = (ii <= jj) if backward else (ii >= jj)
    tri = mask.astype(BF16)
    tri_t = ((ii >= jj) if backward else (ii <= jj)).astype(BF16)

    dt = dt_ref[...]
    dtt = dtt_ref[...]
    a_col = dt * (-jnp.exp(alog_row_ref[...]))
    a_row = dtt * (-jnp.exp(alog_col_ref[...]))
    cum_col = sum(_dot(tri, p) for p in _split3(a_col))
    cum_row = sum(_dot(p, tri_t) for p in _split3(a_row))
    last = 0 if backward else q - 1
    cum_last_col = cum_row[:, last:last + 1]
    w_row = jnp.exp(cum_last_col - cum_row) * dtt
    e_last_row = jnp.exp(cum_col[last:last + 1, :])

    lane = lax.broadcasted_iota(jnp.int32, (q, LANES), 1)
    low_half = lane < SSM_HEAD_DIM
    base = 32 * (1 if backward else 0)

    def group(g):
        bg = b_ref[:, g * n_state:(g + 1) * n_state]
        cg = c_ref[:, g * n_state:(g + 1) * n_state]
        cb = _dot_nt(cg, bg)
        bt = bg.astype(F32).T
        cg32 = cg.astype(F32)
        for pair in range(hpg // 2):
            slot = g * (hpg // 2) + pair
            xp = x_ref[:, slot * LANES:(slot + 1) * LANES]
            hs = st_ref[slot]
            rhs = jnp.concatenate([xp, hs.astype(BF16)], axis=0)
            ys, ss, els = [], [], []
            for r in range(2):
                hl = base + g * hpg + pair * 2 + r
                cc = jnp.broadcast_to(cum_col[:, hl:hl + 1], (q, LANES))
                cr = cum_row[hl:hl + 1, :]
                seg = jnp.exp(jnp.where(mask, cc - cr, NEG_BIG))
                m = (cb * seg * dtt[hl:hl + 1, :]).astype(BF16)
                ce = (cg32 * jnp.exp(cc)).astype(BF16)
                ys.append(_dot(jnp.concatenate([m, ce], axis=1), rhs))
                bts = (bt * w_row[hl:hl + 1, :]).astype(BF16)
                ss.append(_dot(bts, xp))
                els.append(e_last_row[:, hl:hl + 1])
            y_ref[:, slot * LANES:(slot + 1) * LANES] = jnp.where(
                low_half, ys[0], ys[1]).astype(y_ref.dtype)
            low_n = low_half[:n_state]
            decay = jnp.where(low_n, els[0], els[1])
            st_ref[slot] = hs * decay + jnp.where(low_n, ss[0], ss[1])

    return group


def _ssd_kernel(xf_ref, bf_ref, cf_ref, dtf_ref, dttf_ref,
                xb_ref, bb_ref, cb_ref, dtb_ref, dttb_ref,
                alog_row_ref, alog_col_ref, yf_ref, yb_ref, stf_ref, stb_ref):
    @pl.when(pl.program_id(1) == 0)
    def _():
        stf_ref[...] = jnp.zeros(stf_ref.shape, F32)
        stb_ref[...] = jnp.zeros(stb_ref.shape, F32)

    fwd = _ssd_direction(xf_ref, bf_ref, cf_ref, dtf_ref, dttf_ref, alog_row_ref, alog_col_ref,
                         stf_ref, yf_ref, backward=False)
    bwd = _ssd_direction(xb_ref, bb_ref, cb_ref, dtb_ref, dttb_ref, alog_row_ref, alog_col_ref,
                         stb_ref, yb_ref, backward=True)
    for g in range(SSM_GROUPS):
        fwd(g)
        bwd(g)


def _ssd(geom, xbc, dt, dtt, alog_row, alog_col, d_inner, with_ctx_out):
    q = SSM_CHUNK
    n_bc = SSM_GROUPS * SSM_STATE
    lat_chunks = geom.L // q
    ctx_chunks = geom.C // q
    lat_total = geom.rows_lat // q
    steps = lat_chunks + ctx_chunks

    def fwd_chunk(b, s):
        return jnp.where(s < ctx_chunks, lat_total + b * ctx_chunks + s,
                         b * lat_chunks + (s - ctx_chunks))

    def bwd_chunk(b, s):
        return jnp.where(s < ctx_chunks, lat_total + b * ctx_chunks + (ctx_chunks - 1 - s),
                         b * lat_chunks + (lat_chunks - 1 - (s - ctx_chunks)))

    def specs(chunk):
        return [pl.BlockSpec((q, d_inner), lambda b, s: (chunk(b, s), 0)),
                pl.BlockSpec((q, n_bc), lambda b, s: (chunk(b, s), d_inner // n_bc)),
                pl.BlockSpec((q, n_bc), lambda b, s: (chunk(b, s), d_inner // n_bc + 1)),
                pl.BlockSpec((q, LANES), lambda b, s: (chunk(b, s), 0)),
                pl.BlockSpec((LANES, q), lambda b, s: (0, chunk(b, s)))]

    n_rows = geom.rows
    y_shape = jax.ShapeDtypeStruct((n_rows, d_inner), BF16)
    state = pltpu.VMEM((SSM_GROUPS * 2, SSM_STATE, LANES), F32)
    return pl.pallas_call(
        _ssd_kernel,
        grid=(geom.B, steps),
        in_specs=specs(fwd_chunk) + specs(bwd_chunk)
        + [pl.BlockSpec((1, LANES), lambda b, s: (0, 0)),
           pl.BlockSpec((LANES, 1), lambda b, s: (0, 0))],
        out_specs=[pl.BlockSpec((q, d_inner), lambda b, s: (fwd_chunk(b, s), 0)),
                   pl.BlockSpec((q, d_inner), lambda b, s: (bwd_chunk(b, s), 0))],
        out_shape=[y_shape, y_shape],
        scratch_shapes=[state, state],
        compiler_params=_params(2, ("parallel", "arbitrary")),
        name="ssd_scan",
    )(xbc, xbc, xbc, dt, dtt, xbc, xbc, xbc, dt, dtt, alog_row, alog_col)


def _ssm_out_kernel(yf_ref, yb_ref, x_ref, z_ref, dsk_ref, ng_ref, w_ref, h_ref, gt_ref,
                    g_ref, b_ref, o_ref):
    d_inner = yf_ref.shape[1]
    gw = d_inner // SSM_GROUPS
    acc = jnp.zeros(o_ref.shape, F32)
    for c in range(0, d_inner, gw):
        sl = slice(c, c + gw)
        y = (yf_ref[:, sl].astype(F32) + yb_ref[:, sl].astype(F32)
             + dsk_ref[:, sl] * x_ref[:, sl].astype(F32))
        gt = y * _silu(z_ref[:, sl].astype(F32))
        ms = jnp.mean(gt * gt, axis=-1, keepdims=True)
        gn = (gt * lax.rsqrt(ms + RMS_EPS) * ng_ref[:, sl]).astype(BF16)
        acc = acc + _dot(gn, w_ref[sl, :])
    zz = DEEPNORM_ALPHA * h_ref[...] + gt_ref[...] * acc
    o_ref[...] = _layer_norm(zz, g_ref[...], b_ref[...])


def _ssm_out(geom, layer, yf, yb, xbc, z, d_skip, norm_g, out_w, h, mods, ln_g, ln_b,
             n_rows, tm):
    d = geom.D
    d_inner = z.shape[1]
    wide = pl.BlockSpec((tm, d_inner), lambda t: (t, 0))
    row = pl.BlockSpec((tm, d), lambda t: (t, 0))
    return pl.pallas_call(
        _ssm_out_kernel,
        grid=(n_rows // tm,),
        in_specs=[wide, wide, wide, wide, _const_spec((1, d_inner)), _const_spec((1, d_inner)),
                  _const_spec((d_inner, d)), row, geom.mod_spec(layer, 2, tm),
                  _const_spec((1, d)), _const_spec((1, d))],
        out_specs=row,
        out_shape=jax.ShapeDtypeStruct((n_rows, d), F32),
        compiler_params=_params(1),
        name="ssm_out",
    )(yf, yb, xbc, z, d_skip, norm_g, out_w, h, mods, ln_g, ln_b)


def _qkv_kernel(h_ref, sh_ref, sc_ref, wq_ref, wk_ref, wv_ref, pq_ref, pk_ref, gq_ref, gk_ref,
                cos_ref, sin_ref, q_ref, k_ref, v_ref):
    u = (h_ref[...] * (1.0 + sc_ref[...]) + sh_ref[...]).astype(BF16)
    cos = cos_ref[...]
    sin = sin_ref[...]

    def norm_rope(x, p_ref, g_ref):
        n = x.shape[1]
        ms = _dot((x * x).astype(BF16), p_ref[...])
        xn = x * lax.rsqrt(ms + RMS_EPS) * g_ref[...]
        lane = lax.broadcasted_iota(jnp.int32, xn.shape, 1)
        partner = jnp.where(lane % 2 == 0, pltpu.roll(xn, n - 1, axis=1),
                            pltpu.roll(xn, 1, axis=1))
        reps = n // cos.shape[1]
        return xn * jnp.tile(cos, (1, reps)) + partner * jnp.tile(sin, (1, reps))

    q = norm_rope(_dot(u, wq_ref[...]), pq_ref, gq_ref)
    q_ref[...] = (q * (ATTN_HEAD_DIM ** -0.5 * math.log2(math.e))).astype(BF16)
    k_ref[...] = norm_rope(_dot(u, wk_ref[...]), pk_ref, gk_ref).astype(BF16)
    v = _dot(u, wv_ref[...])
    lane = lax.broadcasted_iota(jnp.int32, v.shape, 1)
    v_ref[...] = jnp.where(lane % LANES < ATTN_HEAD_DIM, v, 1.0).astype(BF16)


def _qkv(geom, layer, h, mods, wq, wk2, wv, pq, pk, gq, gk, cos_t, sin_t, n_rows, tm):
    d = geom.D
    nq, nk, nv = wq.shape[1], wk2.shape[1], wv.shape[1]
    n_lat_tiles = geom.rows_lat // tm
    tps = geom.L // tm
    table = pl.BlockSpec((tm, LANES), lambda t: (jnp.where(t < n_lat_tiles, t % tps, tps), 0))
    return pl.pallas_call(
        _qkv_kernel,
        grid=(n_rows // tm,),
        in_specs=[pl.BlockSpec((tm, d), lambda t: (t, 0)),
                  geom.mod_spec(layer, 0, tm), geom.mod_spec(layer, 1, tm),
                  _const_spec((d, nq)), _const_spec((d, nk)), _const_spec((d, nv)),
                  _const_spec((nq, nq)), _const_spec((nk, nk)),
                  _const_spec((1, nq)), _const_spec((1, nk)), table, table],
        out_specs=[pl.BlockSpec((tm, nq), lambda t: (t, 0)),
                   pl.BlockSpec((tm, nk), lambda t: (t, 0)),
                   pl.BlockSpec((tm, nv), lambda t: (t, 0))],
        out_shape=[jax.ShapeDtypeStruct((n_rows, nq), BF16),
                   jax.ShapeDtypeStruct((n_rows, nk), BF16),
                   jax.ShapeDtypeStruct((n_rows, nv), BF16)],
        compiler_params=_params(1),
        name="attn_qkv",
    )(h, mods, mods, wq, wk2, wv, pq, pk, gq, gk, cos_t, sin_t)


def _flash_kernel(q_ref, kl_ref, vl_ref, kc_ref, vc_ref, *rest, n_lat_chunks, ck):
    n_heads = 4
    n_scratch = 4 * n_heads
    o_ref = rest[-n_scratch - 1]
    s_refs = (rest[-n_scratch:-3 * n_heads], rest[-3 * n_heads:-2 * n_heads])
    m_refs = rest[-2 * n_heads:-n_heads]
    acc_refs = rest[-n_heads:]
    tq = q_ref.shape[0]
    lane = lax.broadcasted_iota(jnp.int32, (tq, LANES), 1)
    low_half = lane < ATTN_HEAD_DIM

    qms = []
    for pair in range(n_heads // 2):
        q2 = q_ref[:, pair * LANES:(pair + 1) * LANES].astype(F32)
        qms.append(jnp.where(low_half, q2, 0.0).astype(BF16))
        qms.append(jnp.where(low_half, 0.0, q2).astype(BF16))

    def update_head(h, s, vc):
        m = m_refs[h][...]
        row_max = jnp.max(s, axis=-1, keepdims=True)
        m_new = jnp.maximum(m, jnp.broadcast_to(row_max, m.shape))
        m_wide = jnp.concatenate([m_new] * (s.shape[1] // LANES), axis=1)
        p = jnp.exp2(s - m_wide).astype(BF16)
        acc_refs[h][...] = jnp.exp2(m - m_new) * acc_refs[h][...] + _dot(p, vc)
        m_refs[h][...] = m_new

    def lat_rows(chunk):
        if isinstance(chunk, int):
            return pl.ds(chunk * ck, ck)
        return pl.ds(pl.multiple_of(chunk * ck, ck), ck)

    def step(cur, nxt, next_chunk, cur_chunk):
        kc = kl_ref[lat_rows(next_chunk), :]
        vc = vl_ref[lat_rows(cur_chunk), :]
        for h in range(n_heads):
            s_refs[nxt][h][...] = _dot_nt(qms[h], kc)
            update_head(h, s_refs[cur][h][...], vc)

    for h in range(n_heads):
        m_refs[h][...] = jnp.full((tq, LANES), NEG_BIG, F32)
        acc_refs[h][...] = jnp.zeros((tq, LANES), F32)
    n_ctx = kc_ref.shape[0]
    if n_lat_chunks:
        for h in range(n_heads):
            s_refs[0][h][...] = _dot_nt(qms[h], kl_ref[lat_rows(0), :])

        def body(t, _):
            c0 = 2 * t
            step(0, 1, c0 + 1, c0)
            step(1, 0, c0 + 2, c0 + 1)
            return 0

        lax.fori_loop(0, n_lat_chunks // 2 - 1, body, 0)
        step(0, 1, n_lat_chunks - 1, n_lat_chunks - 2)
        vc = vl_ref[lat_rows(n_lat_chunks - 1), :]
        for h in range(n_heads):
            s_refs[0][h][:, :n_ctx] = _dot_nt(qms[h], kc_ref[...])
            update_head(h, s_refs[1][h][...], vc)
        for h in range(n_heads):
            update_head(h, s_refs[0][h][:, :n_ctx], vc_ref[...])
    else:
        for h in range(n_heads):
            update_head(h, _dot_nt(qms[h], kc_ref[...]), vc_ref[...])

    accs = [r[...] for r in acc_refs]
    outs = [acc * (1.0 / pltpu.roll(acc, ATTN_HEAD_DIM, axis=1)) for acc in accs]
    for pair in range(n_heads // 2):
        o_pair = jnp.where(low_half, outs[2 * pair],
                           pltpu.roll(outs[2 * pair + 1], ATTN_HEAD_DIM, axis=1))
        o_ref[:, pair * LANES:(pair + 1) * LANES] = o_pair.astype(BF16)


def _flash(geom, q, k2, v2, tq, ck, latent_queries, o_prev=None):
    b, l, c = geom.B, geom.L, geom.C
    gw = 4 * ATTN_HEAD_DIM
    ctx_block0 = geom.rows_lat // c
    if latent_queries:
        nq = l // tq
        q_spec = pl.BlockSpec((tq, gw), lambda bi, g, i: (bi * nq + i, g))
        kl_rows, n_lat_chunks = l, l // ck
        kl_map = lambda bi, g, i: (bi, g)
    else:
        nq = 1
        q_spec = pl.BlockSpec((tq, gw), lambda bi, g, i: (ctx_block0 + bi, g))
        kl_rows, n_lat_chunks = ck, 0
        kl_map = lambda bi, g, i: (0, g)
    kl_spec = pl.BlockSpec((kl_rows, LANES), kl_map)
    kc_spec = pl.BlockSpec((c, LANES), lambda bi, g, i: (ctx_block0 + bi, g))
    args = [q, k2, v2, k2, v2]
    in_specs = [q_spec, kl_spec, kl_spec, kc_spec, kc_spec]
    aliases = {}
    if o_prev is not None:
        args.append(o_prev)
        in_specs.append(pl.BlockSpec(memory_space=pl.ANY))
        aliases = {len(args) - 1: 0}
    return pl.pallas_call(
        functools.partial(_flash_kernel, n_lat_chunks=n_lat_chunks, ck=ck),
        grid=(b, ATTN_KV_HEADS, nq),
        in_specs=in_specs,
        out_specs=q_spec,
        out_shape=jax.ShapeDtypeStruct(q.shape, BF16),
        input_output_aliases=aliases,
        scratch_shapes=(
            [pltpu.VMEM((tq, ck) if n_lat_chunks else (SUBLANES, LANES), F32)] * 8
            + [pltpu.VMEM((tq, LANES), F32)] * 8),
        compiler_params=_params(3),
        name="flash_lat" if latent_queries else "flash_ctx",
    )(*args)


VT_ROWS = ATTN_HEAD_DIM + BF16_ROWS


def _flash_t_kernel(q_ref, kl_ref, vtl_ref, kc_ref, vtc_ref, *rest, n_lat_chunks, ck):
    n_heads = 4
    n_scratch = 4 * n_heads
    o_ref = rest[-n_scratch - 1]
    s_refs = (rest[-n_scratch:-3 * n_heads], rest[-3 * n_heads:-2 * n_heads])
    m_refs = rest[-2 * n_heads:-n_heads]
    acc_refs = rest[-n_heads:]
    tq = q_ref.shape[0]
    row = lax.broadcasted_iota(jnp.int32, (LANES, tq), 0)
    low_half = row < ATTN_HEAD_DIM

    qts = []
    for pair in range(n_heads // 2):
        q2t = q_ref[:, pair * LANES:(pair + 1) * LANES].astype(F32).T
        qts.append(jnp.where(low_half, q2t, 0.0).astype(BF16))
        qts.append(jnp.where(low_half, 0.0, q2t).astype(BF16))

    def update_head(h, st, vt):
        m = m_refs[h][...]
        m_new = jnp.maximum(m, jnp.max(st, axis=0, keepdims=True))
        p = jnp.exp2(st - m_new).astype(BF16)
        acc_refs[h][...] = jnp.exp2(m - m_new) * acc_refs[h][...] + _dot(vt, p)
        m_refs[h][...] = m_new

    def lat_rows(chunk):
        return pl.ds(pl.multiple_of(chunk * ck, ck), ck)

    def step(cur, nxt, next_chunk, cur_chunk):
        kc = kl_ref[lat_rows(next_chunk), :]
        vt = vtl_ref[cur_chunk]
        for h in range(n_heads):
            s_refs[nxt][h][...] = _dot(kc, qts[h])
            update_head(h, s_refs[cur][h][...], vt)

    for h in range(n_heads):
        m_refs[h][...] = jnp.full((1, tq), NEG_BIG, F32)
        acc_refs[h][...] = jnp.zeros((VT_ROWS, tq), F32)
    if n_lat_chunks:
        for h in range(n_heads):
            s_refs[0][h][...] = _dot(kl_ref[lat_rows(0), :], qts[h])
    for h in range(n_heads):
        update_head(h, _dot(kc_ref[...], qts[h]), vtc_ref[...])
    if n_lat_chunks:
        def body(t, _):
            c0 = 2 * t
            step(0, 1, c0 + 1, c0)
            step(1, 0, jnp.minimum(c0 + 2, n_lat_chunks - 1), c0 + 1)
            return 0

        lax.fori_loop(0, n_lat_chunks // 2, body, 0)

    outs_t = []
    for h in range(n_heads):
        acc = acc_refs[h][...]
        outs_t.append(acc[:ATTN_HEAD_DIM] * (1.0 / acc[ATTN_HEAD_DIM:ATTN_HEAD_DIM + 1]))
    for pair in range(n_heads // 2):
        o_t = jnp.concatenate([outs_t[2 * pair], outs_t[2 * pair + 1]], axis=0)
        o_ref[:, pair * LANES:(pair + 1) * LANES] = o_t.T.astype(BF16)


def _value_rows(geom, v, ck):
    n_kv, hd = ATTN_KV_HEADS, ATTN_HEAD_DIM
    n_rows = v.shape[0]
    vt = v.T.reshape(n_kv, hd, n_rows)
    vt = jnp.concatenate([vt, jnp.ones((n_kv, BF16_ROWS, n_rows), v.dtype)], axis=1)
    lat = vt[:, :, :geom.rows_lat].reshape(n_kv, VT_ROWS, geom.B, geom.L // ck, ck)
    ctx = vt[:, :, geom.rows_lat:].reshape(n_kv, VT_ROWS, geom.B, geom.C)
    return lat.transpose(2, 0, 3, 1, 4), ctx.transpose(2, 0, 1, 3)


def _flash_t(geom, q, k2, vt_lat, vt_ctx, tq, ck, latent_queries, o_prev=None):
    b, l, c = geom.B, geom.L, geom.C
    gw = 4 * ATTN_HEAD_DIM
    ctx_block0 = geom.rows_lat // c
    chunks = l // ck
    if latent_queries:
        nq = l // tq
        q_spec = pl.BlockSpec((tq, gw), lambda bi, g, i: (bi * nq + i, g))
        kl_rows, n_lat_chunks = l, chunks
        kl_map = lambda bi, g, i: (bi, g)
        vtl_map = lambda bi, g, i: (bi, g, 0, 0, 0)
    else:
        nq = 1
        q_spec = pl.BlockSpec((tq, gw), lambda bi, g, i: (ctx_block0 + bi, g))
        kl_rows, n_lat_chunks = ck, 0
        kl_map = lambda bi, g, i: (0, g)
        vtl_map = lambda bi, g, i: (0, g, 0, 0, 0)
    kl_spec = pl.BlockSpec((kl_rows, LANES), kl_map)
    vtl_spec = pl.BlockSpec((None, None, chunks, VT_ROWS, ck), vtl_map)
    kc_spec = pl.BlockSpec((c, LANES), lambda bi, g, i: (ctx_block0 + bi, g))
    vtc_spec = pl.BlockSpec((None, None, VT_ROWS, c), lambda bi, g, i: (bi, g, 0, 0))
    args = [q, k2, vt_lat, k2, vt_ctx]
    in_specs = [q_spec, kl_spec, vtl_spec, kc_spec, vtc_spec]
    aliases = {}
    if o_prev is not None:
        args.append(o_prev)
        in_specs.append(pl.BlockSpec(memory_space=pl.ANY))
        aliases = {len(args) - 1: 0}
    return pl.pallas_call(
        functools.partial(_flash_t_kernel, n_lat_chunks=n_lat_chunks, ck=ck),
        grid=(b, ATTN_KV_HEADS, nq),
        in_specs=in_specs,
        out_specs=q_spec,
        out_shape=jax.ShapeDtypeStruct(q.shape, BF16),
        input_output_aliases=aliases,
        scratch_shapes=(
            [pltpu.VMEM((ck, tq) if n_lat_chunks else (SUBLANES, LANES), F32)] * 8
            + [pltpu.VMEM((1, tq), F32)] * 4 + [pltpu.VMEM((VT_ROWS, tq), F32)] * 4),
        compiler_params=_params(3),
        name="flash_lat" if latent_queries else "flash_ctx",
    )(*args)


def _rope_tables(seq, tm):
    rows = seq // GRID_W
    row = jnp.repeat(jnp.arange(rows, dtype=F32), GRID_W)
    col = jnp.tile(jnp.arange(GRID_W, dtype=F32), rows)
    axis_dim = ATTN_HEAD_DIM // 2
    inv_freq = ROPE_THETA ** (-jnp.arange(0, axis_dim, 2, dtype=F32) / axis_dim)
    ang = jnp.concatenate([row[:, None] * inv_freq, col[:, None] * inv_freq], axis=-1)
    cos = jnp.repeat(jnp.cos(ang), 2, axis=-1)
    sign = jnp.tile(jnp.array([-1.0, 1.0], F32), ATTN_HEAD_DIM // 2)
    sin = jnp.repeat(jnp.sin(ang), 2, axis=-1) * sign
    reps = LANES // ATTN_HEAD_DIM
    cos = jnp.concatenate([jnp.tile(cos, (1, reps)), jnp.ones((tm, LANES), F32)], axis=0)
    sin = jnp.concatenate([jnp.tile(sin, (1, reps)), jnp.zeros((tm, LANES), F32)], axis=0)
    return cos, sin


def _head_mean_matrix(n, head_dim):
    idx = jnp.arange(n) // head_dim
    return jnp.where(idx[:, None] == idx[None, :], 1.0 / head_dim, 0.0).astype(BF16)


def kernel(x, c, ctx, c_ctx, ada_w, ada_b, ln_mix_g, ln_mix_b, ln_ffn_g, ln_ffn_b, ffn_w1, ffn_w3, ffn_w2, conv_pw1_w, conv_pw1_b, conv_dw_w, conv_dw_b, conv_norm_g, conv_norm_b, conv_pw2_w, conv_pw2_b, ssm_in_w, ssm_conv_w, ssm_conv_b, ssm_a_log, ssm_dt_bias, ssm_d, ssm_norm_g, ssm_out_w, attn_qkv_w, attn_q_norm, attn_k_norm, attn_out_w):
    batch, seq, d = x.shape
    ctx_len = ctx.shape[1]
    geom = _Geom(batch, seq, ctx_len, d)
    tm = math.gcd(512, math.gcd(seq, batch * ctx_len))
    tmc = math.gcd(256, math.gcd(seq, ctx_len))
    assert batch < MOD_ROWS and seq % GRID_W == 0 and ctx_len % SSM_CHUNK == 0

    cond = jnp.zeros((MOD_ROWS, d), F32).at[:batch].set(c).at[batch].set(c_ctx)
    mods = _modulation_table(cond, ada_w, ada_b)

    h = jnp.concatenate([x.reshape(batch * seq, d), ctx.reshape(batch * ctx_len, d)], axis=0)
    vec = lambda a: a.reshape(1, -1)

    for i in range(DEPTH):
        last = i == DEPTH - 1
        kind, j = i % N_MIXERS, i // N_MIXERS
        n_rows = geom.rows_lat if (last and kind == 0) else h.shape[0]
        n_rows = min(n_rows, h.shape[0])
        g_mix, b_mix = vec(ln_mix_g[i]), vec(ln_mix_b[i])

        if kind == 0:
            h = _conv_layer(geom, i, h, mods, conv_pw1_w[j].astype(BF16), vec(conv_pw1_b[j]),
                            conv_dw_w[j], vec(conv_dw_b[j]), vec(conv_norm_g[j]),
                            vec(conv_norm_b[j]), conv_pw2_w[j].astype(BF16), vec(conv_pw2_b[j]),
                            g_mix, b_mix, n_rows, tmc)
        elif kind == 1:
            in_w = ssm_in_w[j]
            d_inner = ssm_norm_g.shape[1]
            n_xbc = ssm_conv_w.shape[2]
            n_heads2 = ssm_a_log.shape[1]
            wz = in_w[:, :d_inner].astype(BF16)
            wx = in_w[:, d_inner:d_inner + n_xbc].astype(BF16)
            wd = jnp.pad(in_w[:, d_inner + n_xbc:], ((0, 0), (0, LANES - n_heads2)))
            wd_hi = wd.astype(BF16)
            wd_lo = (wd - wd_hi.astype(F32)).astype(BF16)
            dt_bias = jnp.pad(ssm_dt_bias[j], (0, LANES - n_heads2)).reshape(1, LANES)
            ssm_args = (geom, i, h, mods, wz, wx, wd_hi, wd_lo, dt_bias, ssm_conv_w[j],
                        vec(ssm_conv_b[j]))
            outs = _ssm_in(*ssm_args, tm, 0, geom.rows_lat // tm)
            z, xbc, dt = _ssm_in(*ssm_args, tmc, geom.rows_lat, (n_rows - geom.rows_lat) // tmc,
                                 prev_out=outs)
            a_log = jnp.pad(ssm_a_log[j], (0, LANES - n_heads2), constant_values=NEG_BIG)
            yf, yb = _ssd(geom, xbc, dt, dt.T, a_log.reshape(1, LANES), a_log.reshape(LANES, 1),
                          d_inner, True)
            d_skip = jnp.repeat(ssm_d[j], SSM_HEAD_DIM).reshape(1, d_inner)
            h = _ssm_out(geom, i, yf, yb, xbc, z, d_skip, vec(ssm_norm_g[j]),
                         ssm_out_w[j].astype(BF16), h, mods, g_mix, b_mix, n_rows, tm)
        else:
            w = attn_qkv_w[j]
            hd = ATTN_HEAD_DIM
            n_q = attn_out_w.shape[1]
            n_kv = (w.shape[1] - n_q) // 2
            wq = w[:, :n_q].astype(BF16)
            wk = w[:, n_q:n_q + n_kv].reshape(d, ATTN_KV_HEADS, 1, hd)
            wk2 = jnp.broadcast_to(wk, (d, ATTN_KV_HEADS, 2, hd)).reshape(d, 2 * n_kv)
            wv = w[:, n_q + n_kv:].reshape(d, ATTN_KV_HEADS, 1, hd)
            wv2 = jnp.concatenate([wv, jnp.zeros_like(wv)], axis=2).reshape(d, 2 * n_kv)
            gq = jnp.tile(attn_q_norm[j], n_q // hd).reshape(1, n_q)
            gk = jnp.tile(attn_k_norm[j], 2 * n_kv // hd).reshape(1, 2 * n_kv)
            cos_t, sin_t = _rope_tables(seq, tm)
            q, k2, v2 = _qkv(geom, i, h, mods, wq, wk2.astype(BF16), wv2.astype(BF16),
                             _head_mean_matrix(n_q, hd), _head_mean_matrix(2 * n_kv, hd),
                             gq, gk, cos_t, sin_t, n_rows, tm)
            tq = math.gcd(512, seq)
            ck = math.gcd(512, seq // 2)
            o = _flash(geom, q, k2, v2, tq, ck, True)
            o = _flash(geom, q, k2, v2, ctx_len, ck, False, o_prev=o)
            h = _proj_res(geom, i, o, attn_out_w[j].astype(BF16), h, mods, g_mix, b_mix,
                          n_rows, tm)

        h = _ffn(geom, i, h, mods, ffn_w1[i].astype(BF16), ffn_w3[i].astype(BF16),
                 ffn_w2[i].astype(BF16), vec(ln_ffn_g[i]), vec(ln_ffn_b[i]), n_rows, tm)

    return h[:geom.rows_lat].reshape(batch, seq, d)
```

```python
import functools
import math

import jax
import jax.numpy as jnp
from jax import lax
from jax.experimental import pallas as pl
from jax.experimental.pallas import tpu as pltpu

F32 = jnp.float32
BF16 = jnp.bfloat16

DEPTH = 4
GRID_W = 64
N_MIXERS = 3
DEEPNORM_ALPHA = (2 * DEPTH) ** 0.25
LN_EPS = 1e-5
RMS_EPS = 1e-6
CONV_WIDTH = 31
SSM_HEAD_DIM = 64
SSM_GROUPS = 8
SSM_STATE = 128
SSM_CONV_WIDTH = 5
SSM_CHUNK = 128
ATTN_HEAD_DIM = 64
ATTN_KV_HEADS = 4
ROPE_THETA = 10000.0

LANES = 128
SUBLANES = 8
BF16_ROWS = 16
VMEM_LIMIT = 56 * 1024 * 1024

MOD_ROWS = 8
NEG_BIG = -1e30


def _dot(a, b):
    return jnp.dot(a, b, preferred_element_type=F32)


def _dot_nt(a, b):
    return lax.dot_general(a, b, (((1,), (1,)), ((), ())), preferred_element_type=F32)


def _sigmoid(x):
    return 1.0 / (1.0 + jnp.exp(-x))


def _silu(x):
    return x * _sigmoid(x)


def _layer_norm(z, g, b):
    mu = jnp.mean(z, axis=-1, keepdims=True)
    d = z - mu
    var = jnp.mean(d * d, axis=-1, keepdims=True)
    return d * lax.rsqrt(var + LN_EPS) * g + b


def _split3(a):
    a1 = a.astype(BF16)
    r1 = a - a1.astype(F32)
    a2 = r1.astype(BF16)
    a3 = (r1 - a2.astype(F32)).astype(BF16)
    return a1, a2, a3


def _const_spec(shape):
    nd = len(shape)
    return pl.BlockSpec(shape, lambda *_: (0,) * nd, pipeline_mode=pl.Buffered(1))


def _params(n_grid_axes, sem=None):
    return pltpu.CompilerParams(
        dimension_semantics=sem or ("parallel",) * n_grid_axes,
        vmem_limit_bytes=VMEM_LIMIT)


class _Geom:
    def __init__(self, batch, seq, ctx_len, d_model):
        self.B, self.L, self.C, self.D = batch, seq, ctx_len, d_model
        self.rows_lat = batch * seq
        self.rows = batch * (seq + ctx_len)

    def mod_spec(self, layer, which, tm, tile0=0):
        n_lat_tiles = self.rows_lat // tm
        tiles_per_seq = self.L // tm
        b = self.B

        def imap(t, *_):
            t = t + tile0
            row = jnp.where(t < n_lat_tiles, t // tiles_per_seq, b)
            return ((layer * MOD_ROWS + row) * 6 + which, 0, 0)

        return pl.BlockSpec((None, 1, self.D), imap)


def _mod_kernel(cond_ref, w_ref, b_ref, o_ref):
    s = _silu(cond_ref[...])
    w = w_ref[...]
    s_hi = s.astype(BF16)
    s_lo = (s - s_hi.astype(F32)).astype(BF16)
    w_hi = w.astype(BF16)
    w_lo = (w - w_hi.astype(F32)).astype(BF16)
    acc = _dot(s_hi, w_hi) + _dot(s_lo, w_hi) + _dot(s_hi, w_lo)
    o_ref[...] = acc + b_ref[...]


def _modulation_table(cond, ada_w, ada_b):
    depth, d, n = ada_w.shape
    tn = n // 4
    out = pl.pallas_call(
        _mod_kernel,
        grid=(depth, n // tn),
        in_specs=[
            pl.BlockSpec((MOD_ROWS, d), lambda l, j: (0, 0)),
            pl.BlockSpec((None, d, tn), lambda l, j: (l, 0, j)),
            pl.BlockSpec((None, 1, tn), lambda l, j: (l, 0, j)),
        ],
        out_specs=pl.BlockSpec((None, MOD_ROWS, tn), lambda l, j: (l, 0, j)),
        out_shape=jax.ShapeDtypeStruct((depth, MOD_ROWS, n), F32),
        compiler_params=_params(2),
        name="mod_table",
    )(cond, ada_w, ada_b.reshape(depth, 1, n))
    return out.reshape(depth * MOD_ROWS * 6, 1, d)


def _ffn_kernel(h_ref, sh_ref, sc_ref, gt_ref, w1_ref, w3_ref, w2_ref, g_ref, b_ref, o_ref,
                *, chunk):
    h = h_ref[...]
    u = (h * (1.0 + sc_ref[...]) + sh_ref[...]).astype(BF16)
    hidden = w1_ref.shape[1]
    acc = jnp.zeros(h.shape, F32)
    for c in range(0, hidden, chunk):
        a = _dot(u, w1_ref[:, c:c + chunk])
        b = _dot(u, w3_ref[:, c:c + chunk])
        g = (_silu(a) * b).astype(BF16)
        acc = acc + _dot(g, w2_ref[c:c + chunk, :])
    z = DEEPNORM_ALPHA * h + gt_ref[...] * acc
    o_ref[...] = _layer_norm(z, g_ref[...], b_ref[...])


def _ffn(geom, layer, h, mods, w1, w3, w2, ln_g, ln_b, n_rows, tm):
    d = geom.D
    hidden = w1.shape[1]
    row = pl.BlockSpec((tm, d), lambda t: (t, 0))
    return pl.pallas_call(
        functools.partial(_ffn_kernel, chunk=2 * LANES),
        grid=(n_rows // tm,),
        in_specs=[row, geom.mod_spec(layer, 3, tm), geom.mod_spec(layer, 4, tm),
                  geom.mod_spec(layer, 5, tm),
                  _const_spec((d, hidden)), _const_spec((d, hidden)), _const_spec((hidden, d)),
                  _const_spec((1, d)), _const_spec((1, d))],
        out_specs=row,
        out_shape=jax.ShapeDtypeStruct((n_rows, d), F32),
        compiler_params=_params(1),
        name="ffn",
    )(h, mods, mods, mods, w1, w3, w2, ln_g, ln_b)


def _proj_res_kernel(x_ref, w_ref, h_ref, gt_ref, g_ref, b_ref, o_ref):
    y = _dot(x_ref[...], w_ref[...])
    z = DEEPNORM_ALPHA * h_ref[...] + gt_ref[...] * y
    o_ref[...] = _layer_norm(z, g_ref[...], b_ref[...])


def _proj_res(geom, layer, x, w, h, mods, ln_g, ln_b, n_rows, tm):
    d = geom.D
    k = x.shape[1]
    return pl.pallas_call(
        _proj_res_kernel,
        grid=(n_rows // tm,),
        in_specs=[pl.BlockSpec((tm, k), lambda t: (t, 0)), _const_spec((k, d)),
                  pl.BlockSpec((tm, d), lambda t: (t, 0)), geom.mod_spec(layer, 2, tm),
                  _const_spec((1, d)), _const_spec((1, d))],
        out_specs=pl.BlockSpec((tm, d), lambda t: (t, 0)),
        out_shape=jax.ShapeDtypeStruct((n_rows, d), F32),
        compiler_params=_params(1),
        name="proj_res",
    )(x, w, h, mods, ln_g, ln_b)


def _seq_edges(t, tm, geom):
    n_lat_tiles = geom.rows_lat // tm
    lat_tps = geom.L // tm
    ctx_tps = max(geom.C // tm, 1)
    in_lat = t < n_lat_tiles
    pos = jnp.where(in_lat, t % lat_tps, (t - n_lat_tiles) % ctx_tps)
    tps = jnp.where(in_lat, lat_tps, ctx_tps)
    return pos == 0, pos == tps - 1


def _store_window(win_ref, cols, value, halo, tm, is_start, is_end):
    win_ref[:, cols] = value
    win_ref[0:halo, cols] = jnp.where(is_start, 0.0, value[0:halo])
    win_ref[halo + tm:, cols] = jnp.where(is_end, 0.0, value[halo + tm:])


def _halo_specs(tm, halo, n_cols, n_rows_total, tile0=0):
    per = tm // halo
    last = n_rows_total // halo - 1
    prev = pl.BlockSpec((halo, n_cols), lambda t: (jnp.maximum((t + tile0) * per - 1, 0), 0))
    cur = pl.BlockSpec((tm, n_cols), lambda t: (t + tile0, 0))
    nxt = pl.BlockSpec((halo, n_cols), lambda t: (jnp.minimum((t + tile0 + 1) * per, last), 0))
    return prev, cur, nxt


def _depthwise(win_ref, w_ref, width, first, tm, cols, emit, row_block=64):
    phases = {}
    for k in range(width):
        a, b = divmod(first + k, SUBLANES)
        phases.setdefault(b, []).append((a, k))
    for c in range(cols.start, cols.stop, LANES):
        for r in range(0, tm, row_block):
            acc = None
            for b, taps in sorted(phases.items()):
                z = None
                for a, k in taps:
                    lo = r + SUBLANES * a
                    term = (win_ref[lo:lo + row_block + SUBLANES, c:c + LANES]
                            * w_ref[k:k + 1, c:c + LANES])
                    z = term if z is None else z + term
                zb = z[b:b + row_block]
                acc = zb if acc is None else acc + zb
            emit(r, c, acc)


MXU_COLS = 2 * LANES


CONV_HALO = 16


def _conv_layer_kernel(hp_ref, hc_ref, hn_ref, sh_ref, sc_ref, gt_ref, wa_ref, wg_ref, ba_ref,
                       bg_ref, dww_ref, dwb_ref, ng_ref, nb_ref, w2_ref, b2_ref, g_ref, b_ref,
                       o_ref, win_ref, cv_ref, *, geom, tm):
    halo = CONV_HALO
    is_start, is_end = _seq_edges(pl.program_id(0), tm, geom)
    hwin = jnp.concatenate([hp_ref[...], hc_ref[...], hn_ref[...]], axis=0)
    u = (hwin * (1.0 + sc_ref[...]) + sh_ref[...]).astype(BF16)

    def glu_block(cols):
        a = _dot(u, wa_ref[:, cols]) + ba_ref[:, cols]
        g = _dot(u, wg_ref[:, cols]) + bg_ref[:, cols]
        _store_window(win_ref, cols, a * _sigmoid(g), halo, tm, is_start, is_end)

    def emit(r, c, acc):
        cv_ref[r:r + acc.shape[0], c:c + LANES] = acc + dwb_ref[:, c:c + LANES]

    blocks = [slice(c, c + MXU_COLS) for c in range(0, geom.D, MXU_COLS)]
    glu_block(blocks[0])
    for j, cols in enumerate(blocks):
        if j + 1 < len(blocks):
            glu_block(blocks[j + 1])
        _depthwise(win_ref, dww_ref, CONV_WIDTH, halo - (CONV_WIDTH - 1) // 2, tm, cols, emit)
    v = _silu(_layer_norm(cv_ref[...], ng_ref[...], nb_ref[...])).astype(BF16)
    y = _dot(v, w2_ref[...]) + b2_ref[...]
    z = DEEPNORM_ALPHA * hc_ref[...] + gt_ref[...] * y
    o_ref[...] = _layer_norm(z, g_ref[...], b_ref[...])


def _conv_layer(geom, layer, h, mods, w1, b1, dw_w, dw_b, n_g, n_b, w2, b2, ln_g, ln_b,
                n_rows, tm):
    d = geom.D
    prev, cur, nxt = _halo_specs(tm, CONV_HALO, d, h.shape[0])
    half = lambda j: pl.BlockSpec((d, d), lambda t: (0, j), pipeline_mode=pl.Buffered(1))
    bhalf = lambda j: pl.BlockSpec((1, d), lambda t: (0, j), pipeline_mode=pl.Buffered(1))
    vec = _const_spec((1, d))
    return pl.pallas_call(
        functools.partial(_conv_layer_kernel, geom=geom, tm=tm),
        grid=(n_rows // tm,),
        in_specs=[prev, cur, nxt, geom.mod_spec(layer, 0, tm), geom.mod_spec(layer, 1, tm),
                  geom.mod_spec(layer, 2, tm), half(0), half(1), bhalf(0), bhalf(1),
                  _const_spec((CONV_WIDTH, d)), vec, vec, vec, _const_spec((d, d)), vec, vec, vec],
        out_specs=pl.BlockSpec((tm, d), lambda t: (t, 0)),
        out_shape=jax.ShapeDtypeStruct((n_rows, d), F32),
        scratch_shapes=[pltpu.VMEM((tm + 2 * CONV_HALO, d), F32), pltpu.VMEM((tm, d), F32)],
        compiler_params=_params(1),
        name="conv_layer",
    )(h, h, h, mods, mods, mods, w1, w1, b1, b1, dw_w, dw_b, n_g, n_b, w2, b2, ln_g, ln_b)


SSM_HALO = 8


def _ssm_in_kernel(hp_ref, hc_ref, hn_ref, sh_ref, sc_ref, wz_ref, wx_ref, wdh_ref, wdl_ref,
                   dtb_ref, cw_ref, cb_ref, *rest, geom, tm, tile0, col_chunk):
    z_ref, x_ref, dt_ref, win_ref = rest[-4:]
    halo = SSM_HALO
    is_start, is_end = _seq_edges(pl.program_id(0) + tile0, tm, geom)
    hwin = jnp.concatenate([hp_ref[...], hc_ref[...], hn_ref[...]], axis=0)
    uf = hwin * (1.0 + sc_ref[...]) + sh_ref[...]
    u = uf.astype(BF16)
    u_lo = (uf - u.astype(F32)).astype(BF16)
    own = slice(halo, halo + tm)

    def project_x(cols):
        _store_window(win_ref, cols, _dot(u, wx_ref[:, cols]), halo, tm, is_start, is_end)

    def emit(r, c, acc):
        col = slice(c, c + LANES)
        x_ref[r:r + acc.shape[0], col] = _silu(acc + cb_ref[:, col]).astype(BF16)

    def conv_x(cols):
        _depthwise(win_ref, cw_ref, SSM_CONV_WIDTH, halo - (SSM_CONV_WIDTH - 1) // 2, tm, cols,
                   emit)

    def project_z(cols):
        z_ref[:, cols] = _dot(u, wz_ref[:, cols])[own].astype(BF16)

    def project_dt():
        d = (_dot(u, wdh_ref[...]) + _dot(u_lo, wdh_ref[...]) + _dot(u, wdl_ref[...]))[own]
        d = d + dtb_ref[...]
        dt_ref[...] = jnp.maximum(d, 0.0) + jnp.log1p(jnp.exp(-jnp.abs(d)))

    x_blocks = [slice(c, c + col_chunk) for c in range(0, wx_ref.shape[1], col_chunk)]
    z_blocks = [slice(c, c + col_chunk) for c in range(0, wz_ref.shape[1], col_chunk)]
    extra = [functools.partial(project_z, cols) for cols in z_blocks] + [project_dt]
    per_conv = -(-len(extra) // len(x_blocks))
    project_x(x_blocks[0])
    for j, cols in enumerate(x_blocks):
        if j + 1 < len(x_blocks):
            project_x(x_blocks[j + 1])
        for f in extra[j * per_conv:(j + 1) * per_conv]:
            f()
        conv_x(cols)


def _ssm_in(geom, layer, h, mods, wz, wx, wd_hi, wd_lo, dt_bias, conv_w, conv_b, tm, row0,
            n_tiles, prev_out=None):
    d = geom.D
    nz, nx = wz.shape[1], wx.shape[1]
    n_rows = h.shape[0]
    tile0 = row0 // tm
    col_chunk = MXU_COLS
    prev, cur, nxt = _halo_specs(tm, SSM_HALO, d, n_rows, tile0)
    args = [h, h, h, mods, mods, wz, wx, wd_hi, wd_lo, dt_bias, conv_w, conv_b]
    in_specs = [prev, cur, nxt, geom.mod_spec(layer, 0, tm, tile0),
                geom.mod_spec(layer, 1, tm, tile0),
                _const_spec((d, nz)), _const_spec((d, nx)),
                _const_spec((d, LANES)), _const_spec((d, LANES)), _const_spec((1, LANES)),
                _const_spec((SSM_CONV_WIDTH, nx)), _const_spec((1, nx))]
    aliases = {}
    if prev_out is not None:
        for o in prev_out:
            aliases[len(args)] = len(aliases)
            args.append(o)
            in_specs.append(pl.BlockSpec(memory_space=pl.ANY))
    return pl.pallas_call(
        functools.partial(_ssm_in_kernel, geom=geom, tm=tm, tile0=tile0, col_chunk=col_chunk),
        grid=(n_tiles,),
        in_specs=in_specs,
        out_specs=[pl.BlockSpec((tm, nz), lambda t: (t + tile0, 0)),
                   pl.BlockSpec((tm, nx), lambda t: (t + tile0, 0)),
                   pl.BlockSpec((tm, LANES), lambda t: (t + tile0, 0))],
        out_shape=[jax.ShapeDtypeStruct((n_rows, nz), BF16),
                   jax.ShapeDtypeStruct((n_rows, nx), BF16),
                   jax.ShapeDtypeStruct((n_rows, LANES), F32)],
        input_output_aliases=aliases,
        scratch_shapes=[pltpu.VMEM((tm + 2 * SSM_HALO, nx), F32)],
        compiler_params=_params(1),
        name="ssm_in",
    )(*args)


def _ssd_direction(x_ref, b_ref, c_ref, dt_ref, dtt_ref, alog_row_ref, alog_col_ref, st_ref,
                   y_ref, *, backward):
    q = SSM_CHUNK
    n_state = SSM_STATE
    hpg = 4
    ii = lax.broadcasted_iota(jnp.int32, (q, q), 0)
    jj = lax.broadcasted_iota(jnp.int32, (q, q), 1)
    mask = (ii <= jj) if backward else (ii >= jj)
    tri = mask.astype(BF16)
    tri_t = ((ii >= jj) if backward else (ii <= jj)).astype(BF16)

    dt = dt_ref[...]
    dtt = dtt_ref[...]
    a_col = dt * (-jnp.exp(alog_row_ref[...]))
    a_row = dtt * (-jnp.exp(alog_col_ref[...]))
    cum_col = sum(_dot(tri, p) for p in _split3(a_col))
    cum_row = sum(_dot(p, tri_t) for p in _split3(a_row))
    last = 0 if backward else q - 1
    cum_last_col = cum_row[:, last:last + 1]
    w_row = jnp.exp(cum_last_col - cum_row) * dtt
    e_last_row = jnp.exp(cum_col[last:last + 1, :])

    lane = lax.broadcasted_iota(jnp.int32, (q, LANES), 1)
    low_half = lane < SSM_HEAD_DIM
    base = 32 * (1 if backward else 0)

    def group(g):
        bg = b_ref[:, g * n_state:(g + 1) * n_state]
        cg = c_ref[:, g * n_state:(g + 1) * n_state]
        cb = _dot_nt(cg, bg)
        bt = bg.astype(F32).T
        cg32 = cg.astype(F32)
        for pair in range(hpg // 2):
            slot = g * (hpg // 2) + pair
            xp = x_ref[:, slot * LANES:(slot + 1) * LANES]
            hs = st_ref[slot]
            rhs = jnp.concatenate([xp, hs.astype(BF16)], axis=0)
            ys, ss, els = [], [], []
            for r in range(2):
                hl = base + g * hpg + pair * 2 + r
                cc = jnp.broadcast_to(cum_col[:, hl:hl + 1], (q, LANES))
                cr = cum_row[hl:hl + 1, :]
                seg = jnp.exp(jnp.where(mask, cc - cr, NEG_BIG))
                m = (cb * seg * dtt[hl:hl + 1, :]).astype(BF16)
                ce = (cg32 * jnp.exp(cc)).astype(BF16)
                ys.append(_dot(jnp.concatenate([m, ce], axis=1), rhs))
                bts = (bt * w_row[hl:hl + 1, :]).astype(BF16)
                ss.append(_dot(bts, xp))
                els.append(e_last_row[:, hl:hl + 1])
            y_ref[:, slot * LANES:(slot + 1) * LANES] = jnp.where(
                low_half, ys[0], ys[1]).astype(y_ref.dtype)
            low_n = low_half[:n_state]
            decay = jnp.where(low_n, els[0], els[1])
            st_ref[slot] = hs * decay + jnp.where(low_n, ss[0], ss[1])

    return group


def _ssd_kernel(xf_ref, bf_ref, cf_ref, dtf_ref, dttf_ref,
                xb_ref, bb_ref, cb_ref, dtb_ref, dttb_ref,
                alog_row_ref, alog_col_ref, yf_ref, yb_ref, stf_ref, stb_ref):
    @pl.when(pl.program_id(1) == 0)
    def _():
        stf_ref[...] = jnp.zeros(stf_ref.shape, F32)
        stb_ref[...] = jnp.zeros(stb_ref.shape, F32)

    fwd = _ssd_direction(xf_ref, bf_ref, cf_ref, dtf_ref, dttf_ref, alog_row_ref, alog_col_ref,
                         stf_ref, yf_ref, backward=False)
    bwd = _ssd_direction(xb_ref, bb_ref, cb_ref, dtb_ref, dttb_ref, alog_row_ref, alog_col_ref,
                         stb_ref, yb_ref, backward=True)
    for g in range(SSM_GROUPS):
        fwd(g)
        bwd(g)


def _ssd(geom, xbc, dt, dtt, alog_row, alog_col, d_inner, with_ctx_out):
    q = SSM_CHUNK
    n_bc = SSM_GROUPS * SSM_STATE
    lat_chunks = geom.L // q
    ctx_chunks = geom.C // q
    lat_total = geom.rows_lat // q
    steps = lat_chunks + ctx_chunks

    def fwd_chunk(b, s):
        return jnp.where(s < ctx_chunks, lat_total + b * ctx_chunks + s,
                         b * lat_chunks + (s - ctx_chunks))

    def bwd_chunk(b, s):
        return jnp.where(s < ctx_chunks, lat_total + b * ctx_chunks + (ctx_chunks - 1 - s),
                         b * lat_chunks + (lat_chunks - 1 - (s - ctx_chunks)))

    def specs(chunk):
        return [pl.BlockSpec((q, d_inner), lambda b, s: (chunk(b, s), 0)),
                pl.BlockSpec((q, n_bc), lambda b, s: (chunk(b, s), d_inner // n_bc)),
                pl.BlockSpec((q, n_bc), lambda b, s: (chunk(b, s), d_inner // n_bc + 1)),
                pl.BlockSpec((q, LANES), lambda b, s: (chunk(b, s), 0)),
                pl.BlockSpec((LANES, q), lambda b, s: (0, chunk(b, s)))]

    n_rows = geom.rows
    y_shape = jax.ShapeDtypeStruct((n_rows, d_inner), BF16)
    state = pltpu.VMEM((SSM_GROUPS * 2, SSM_STATE, LANES), F32)
    return pl.pallas_call(
        _ssd_kernel,
        grid=(geom.B, steps),
        in_specs=specs(fwd_chunk) + specs(bwd_chunk)
        + [pl.BlockSpec((1, LANES), lambda b, s: (0, 0)),
           pl.BlockSpec((LANES, 1), lambda b, s: (0, 0))],
        out_specs=[pl.BlockSpec((q, d_inner), lambda b, s: (fwd_chunk(b, s), 0)),
                   pl.BlockSpec((q, d_inner), lambda b, s: (bwd_chunk(b, s), 0))],
        out_shape=[y_shape, y_shape],
        scratch_shapes=[state, state],
        compiler_params=_params(2, ("parallel", "arbitrary")),
        name="ssd_scan",
    )(xbc, xbc, xbc, dt, dtt, xbc, xbc, xbc, dt, dtt, alog_row, alog_col)


def _ssm_out_kernel(yf_ref, yb_ref, x_ref, z_ref, dsk_ref, ng_ref, w_ref, h_ref, gt_ref,
                    g_ref, b_ref, o_ref):
    d_inner = yf_ref.shape[1]
    gw = d_inner // SSM_GROUPS
    acc = jnp.zeros(o_ref.shape, F32)
    for c in range(0, d_inner, gw):
        sl = slice(c, c + gw)
        y = (yf_ref[:, sl].astype(F32) + yb_ref[:, sl].astype(F32)
             + dsk_ref[:, sl] * x_ref[:, sl].astype(F32))
        gt = y * _silu(z_ref[:, sl].astype(F32))
        ms = jnp.mean(gt * gt, axis=-1, keepdims=True)
        gn = (gt * lax.rsqrt(ms + RMS_EPS) * ng_ref[:, sl]).astype(BF16)
        acc = acc + _dot(gn, w_ref[sl, :])
    zz = DEEPNORM_ALPHA * h_ref[...] + gt_ref[...] * acc
    o_ref[...] = _layer_norm(zz, g_ref[...], b_ref[...])


def _ssm_out(geom, layer, yf, yb, xbc, z, d_skip, norm_g, out_w, h, mods, ln_g, ln_b,
             n_rows, tm):
    d = geom.D
    d_inner = z.shape[1]
    wide = pl.BlockSpec((tm, d_inner), lambda t: (t, 0))
    row = pl.BlockSpec((tm, d), lambda t: (t, 0))
    return pl.pallas_call(
        _ssm_out_kernel,
        grid=(n_rows // tm,),
        in_specs=[wide, wide, wide, wide, _const_spec((1, d_inner)), _const_spec((1, d_inner)),
                  _const_spec((d_inner, d)), row, geom.mod_spec(layer, 2, tm),
                  _const_spec((1, d)), _const_spec((1, d))],
        out_specs=row,
        out_shape=jax.ShapeDtypeStruct((n_rows, d), F32),
        compiler_params=_params(1),
        name="ssm_out",
    )(yf, yb, xbc, z, d_skip, norm_g, out_w, h, mods, ln_g, ln_b)


def _qkv_kernel(h_ref, sh_ref, sc_ref, wq_ref, wk_ref, wv_ref, pq_ref, pk_ref, gq_ref, gk_ref,
                cos_ref, sin_ref, q_ref, k_ref, v_ref):
    u = (h_ref[...] * (1.0 + sc_ref[...]) + sh_ref[...]).astype(BF16)
    cos = cos_ref[...]
    sin = sin_ref[...]

    def norm_rope(x, p_ref, g_ref):
        n = x.shape[1]
        sq = (x * x).astype(BF16)
        ms = jnp.concatenate([_dot(sq[:, c:c + MXU_COLS], p_ref[...])
                              for c in range(0, n, MXU_COLS)], axis=1)
        xn = x * lax.rsqrt(ms + RMS_EPS) * g_ref[...]
        lane = lax.broadcasted_iota(jnp.int32, xn.shape, 1)
        partner = jnp.where(lane % 2 == 0, pltpu.roll(xn, n - 1, axis=1),
                            pltpu.roll(xn, 1, axis=1))
        reps = n // cos.shape[1]
        return xn * jnp.tile(cos, (1, reps)) + partner * jnp.tile(sin, (1, reps))

    q = norm_rope(_dot(u, wq_ref[...]), pq_ref, gq_ref)
    q_ref[...] = (q * (ATTN_HEAD_DIM ** -0.5 * math.log2(math.e))).astype(BF16)
    k_ref[...] = norm_rope(_dot(u, wk_ref[...]), pk_ref, gk_ref).astype(BF16)
    v = _dot(u, wv_ref[...])
    lane = lax.broadcasted_iota(jnp.int32, v.shape, 1)
    v_ref[...] = jnp.where(lane % LANES < ATTN_HEAD_DIM, v, 1.0).astype(BF16)


def _qkv(geom, layer, h, mods, wq, wk2, wv, pq, pk, gq, gk, cos_t, sin_t, n_rows, tm):
    d = geom.D
    nq, nk, nv = wq.shape[1], wk2.shape[1], wv.shape[1]
    n_lat_tiles = geom.rows_lat // tm
    tps = geom.L // tm
    table = pl.BlockSpec((tm, LANES), lambda t: (jnp.where(t < n_lat_tiles, t % tps, tps), 0))
    return pl.pallas_call(
        _qkv_kernel,
        grid=(n_rows // tm,),
        in_specs=[pl.BlockSpec((tm, d), lambda t: (t, 0)),
                  geom.mod_spec(layer, 0, tm), geom.mod_spec(layer, 1, tm),
                  _const_spec((d, nq)), _const_spec((d, nk)), _const_spec((d, nv)),
                  _const_spec((MXU_COLS, MXU_COLS)), _const_spec((MXU_COLS, MXU_COLS)),
                  _const_spec((1, nq)), _const_spec((1, nk)), table, table],
        out_specs=[pl.BlockSpec((tm, nq), lambda t: (t, 0)),
                   pl.BlockSpec((tm, nk), lambda t: (t, 0)),
                   pl.BlockSpec((tm, nv), lambda t: (t, 0))],
        out_shape=[jax.ShapeDtypeStruct((n_rows, nq), BF16),
                   jax.ShapeDtypeStruct((n_rows, nk), BF16),
                   jax.ShapeDtypeStruct((n_rows, nv), BF16)],
        compiler_params=_params(1),
        name="attn_qkv",
    )(h, mods, mods, wq, wk2, wv, pq, pk, gq, gk, cos_t, sin_t)


def _flash_kernel(q_ref, kl_ref, vl_ref, kc_ref, vc_ref, *rest, n_lat_chunks, ck):
    n_heads = 4
    n_scratch = 4 * n_heads
    o_ref = rest[-n_scratch - 1]
    s_refs = (rest[-n_scratch:-3 * n_heads], rest[-3 * n_heads:-2 * n_heads])
    m_refs = rest[-2 * n_heads:-n_heads]
    acc_refs = rest[-n_heads:]
    tq = q_ref.shape[0]
    lane = lax.broadcasted_iota(jnp.int32, (tq, LANES), 1)
    low_half = lane < ATTN_HEAD_DIM

    qms = []
    for pair in range(n_heads // 2):
        q2 = q_ref[:, pair * LANES:(pair + 1) * LANES].astype(F32)
        qms.append(jnp.where(low_half, q2, 0.0).astype(BF16))
        qms.append(jnp.where(low_half, 0.0, q2).astype(BF16))

    def update_head(h, s, vc):
        m = m_refs[h][...]
        row_max = jnp.max(s, axis=-1, keepdims=True)
        m_new = jnp.maximum(m, jnp.broadcast_to(row_max, m.shape))
        m_wide = jnp.concatenate([m_new] * (s.shape[1] // LANES), axis=1)
        p = jnp.exp2(s - m_wide).astype(BF16)
        acc_refs[h][...] = jnp.exp2(m - m_new) * acc_refs[h][...] + _dot(p, vc)
        m_refs[h][...] = m_new

    def lat_rows(chunk):
        if isinstance(chunk, int):
            return pl.ds(chunk * ck, ck)
        return pl.ds(pl.multiple_of(chunk * ck, ck), ck)

    def step(cur, nxt, next_chunk, cur_chunk):
        kc = kl_ref[lat_rows(next_chunk), :]
        vc = vl_ref[lat_rows(cur_chunk), :]
        for h in range(n_heads):
            s_refs[nxt][h][...] = _dot_nt(qms[h], kc)
            update_head(h, s_refs[cur][h][...], vc)

    for h in range(n_heads):
        m_refs[h][...] = jnp.full((tq, LANES), NEG_BIG, F32)
        acc_refs[h][...] = jnp.zeros((tq, LANES), F32)
    n_ctx = kc_ref.shape[0]
    if n_lat_chunks:
        for h in range(n_heads):
            s_refs[0][h][...] = _dot_nt(qms[h], kl_ref[lat_rows(0), :])

        def body(t, _):
            c0 = 2 * t
            step(0, 1, c0 + 1, c0)
            step(1, 0, c0 + 2, c0 + 1)
            return 0

        lax.fori_loop(0, n_lat_chunks // 2 - 1, body, 0)
        step(0, 1, n_lat_chunks - 1, n_lat_chunks - 2)
        vc = vl_ref[lat_rows(n_lat_chunks - 1), :]
        for h in range(n_heads):
            s_refs[0][h][:, :n_ctx] = _dot_nt(qms[h], kc_ref[...])
            update_head(h, s_refs[1][h][...], vc)
        for h in range(n_heads):
            update_head(h, s_refs[0][h][:, :n_ctx], vc_ref[...])
    else:
        for h in range(n_heads):
            update_head(h, _dot_nt(qms[h], kc_ref[...]), vc_ref[...])

    accs = [r[...] for r in acc_refs]
    outs = [acc * (1.0 / pltpu.roll(acc, ATTN_HEAD_DIM, axis=1)) for acc in accs]
    for pair in range(n_heads // 2):
        o_pair = jnp.where(low_half, outs[2 * pair],
                           pltpu.roll(outs[2 * pair + 1], ATTN_HEAD_DIM, axis=1))
        o_ref[:, pair * LANES:(pair + 1) * LANES] = o_pair.astype(BF16)


def _flash(geom, q, k2, v2, tq, ck, latent_queries, o_prev=None):
    b, l, c = geom.B, geom.L, geom.C
    gw = 4 * ATTN_HEAD_DIM
    ctx_block0 = geom.rows_lat // c
    if latent_queries:
        nq = l // tq
        q_spec = pl.BlockSpec((tq, gw), lambda bi, g, i: (bi * nq + i, g))
        kl_rows, n_lat_chunks = l, l // ck
        kl_map = lambda bi, g, i: (bi, g)
    else:
        nq = 1
        q_spec = pl.BlockSpec((tq, gw), lambda bi, g, i: (ctx_block0 + bi, g))
        kl_rows, n_lat_chunks = ck, 0
        kl_map = lambda bi, g, i: (0, g)
    kl_spec = pl.BlockSpec((kl_rows, LANES), kl_map)
    kc_spec = pl.BlockSpec((c, LANES), lambda bi, g, i: (ctx_block0 + bi, g))
    args = [q, k2, v2, k2, v2]
    in_specs = [q_spec, kl_spec, kl_spec, kc_spec, kc_spec]
    aliases = {}
    if o_prev is not None:
        args.append(o_prev)
        in_specs.append(pl.BlockSpec(memory_space=pl.ANY))
        aliases = {len(args) - 1: 0}
    return pl.pallas_call(
        functools.partial(_flash_kernel, n_lat_chunks=n_lat_chunks, ck=ck),
        grid=(b, ATTN_KV_HEADS, nq),
        in_specs=in_specs,
        out_specs=q_spec,
        out_shape=jax.ShapeDtypeStruct(q.shape, BF16),
        input_output_aliases=aliases,
        scratch_shapes=(
            [pltpu.VMEM((tq, ck) if n_lat_chunks else (SUBLANES, LANES), F32)] * 8
            + [pltpu.VMEM((tq, LANES), F32)] * 8),
        compiler_params=_params(3),
        name="flash_lat" if latent_queries else "flash_ctx",
    )(*args)


VT_ROWS = ATTN_HEAD_DIM + BF16_ROWS


def _flash_t_kernel(q_ref, kl_ref, vtl_ref, kc_ref, vtc_ref, *rest, n_lat_chunks, ck):
    n_heads = 4
    n_scratch = 4 * n_heads
    o_ref = rest[-n_scratch - 1]
    s_refs = (rest[-n_scratch:-3 * n_heads], rest[-3 * n_heads:-2 * n_heads])
    m_refs = rest[-2 * n_heads:-n_heads]
    acc_refs = rest[-n_heads:]
    tq = q_ref.shape[0]
    row = lax.broadcasted_iota(jnp.int32, (LANES, tq), 0)
    low_half = row < ATTN_HEAD_DIM

    qts = []
    for pair in range(n_heads // 2):
        q2t = q_ref[:, pair * LANES:(pair + 1) * LANES].astype(F32).T
        qts.append(jnp.where(low_half, q2t, 0.0).astype(BF16))
        qts.append(jnp.where(low_half, 0.0, q2t).astype(BF16))

    def update_head(h, st, vt):
        m = m_refs[h][...]
        m_new = jnp.maximum(m, jnp.max(st, axis=0, keepdims=True))
        p = jnp.exp2(st - m_new).astype(BF16)
        acc_refs[h][...] = jnp.exp2(m - m_new) * acc_refs[h][...] + _dot(vt, p)
        m_refs[h][...] = m_new

    def lat_rows(chunk):
        return pl.ds(pl.multiple_of(chunk * ck, ck), ck)

    def step(cur, nxt, next_chunk, cur_chunk):
        kc = kl_ref[lat_rows(next_chunk), :]
        vt = vtl_ref[cur_chunk]
        for h in range(n_heads):
            s_refs[nxt][h][...] = _dot(kc, qts[h])
            update_head(h, s_refs[cur][h][...], vt)

    for h in range(n_heads):
        m_refs[h][...] = jnp.full((1, tq), NEG_BIG, F32)
        acc_refs[h][...] = jnp.zeros((VT_ROWS, tq), F32)
    if n_lat_chunks:
        for h in range(n_heads):
            s_refs[0][h][...] = _dot(kl_ref[lat_rows(0), :], qts[h])
    for h in range(n_heads):
        update_head(h, _dot(kc_ref[...], qts[h]), vtc_ref[...])
    if n_lat_chunks:
        def body(t, _):
            c0 = 2 * t
            step(0, 1, c0 + 1, c0)
            step(1, 0, jnp.minimum(c0 + 2, n_lat_chunks - 1), c0 + 1)
            return 0

        lax.fori_loop(0, n_lat_chunks // 2, body, 0)

    outs_t = []
    for h in range(n_heads):
        acc = acc_refs[h][...]
        outs_t.append(acc[:ATTN_HEAD_DIM] * (1.0 / acc[ATTN_HEAD_DIM:ATTN_HEAD_DIM + 1]))
    for pair in range(n_heads // 2):
        o_t = jnp.concatenate([outs_t[2 * pair], outs_t[2 * pair + 1]], axis=0)
        o_ref[:, pair * LANES:(pair + 1) * LANES] = o_t.T.astype(BF16)


def _value_rows(geom, v, ck):
    n_kv, hd = ATTN_KV_HEADS, ATTN_HEAD_DIM
    n_rows = v.shape[0]
    vt = v.T.reshape(n_kv, hd, n_rows)
    vt = jnp.concatenate([vt, jnp.ones((n_kv, BF16_ROWS, n_rows), v.dtype)], axis=1)
    lat = vt[:, :, :geom.rows_lat].reshape(n_kv, VT_ROWS, geom.B, geom.L // ck, ck)
    ctx = vt[:, :, geom.rows_lat:].reshape(n_kv, VT_ROWS, geom.B, geom.C)
    return lat.transpose(2, 0, 3, 1, 4), ctx.transpose(2, 0, 1, 3)


def _flash_t(geom, q, k2, vt_lat, vt_ctx, tq, ck, latent_queries, o_prev=None):
    b, l, c = geom.B, geom.L, geom.C
    gw = 4 * ATTN_HEAD_DIM
    ctx_block0 = geom.rows_lat // c
    chunks = l // ck
    if latent_queries:
        nq = l // tq
        q_spec = pl.BlockSpec((tq, gw), lambda bi, g, i: (bi * nq + i, g))
        kl_rows, n_lat_chunks = l, chunks
        kl_map = lambda bi, g, i: (bi, g)
        vtl_map = lambda bi, g, i: (bi, g, 0, 0, 0)
    else:
        nq = 1
        q_spec = pl.BlockSpec((tq, gw), lambda bi, g, i: (ctx_block0 + bi, g))
        kl_rows, n_lat_chunks = ck, 0
        kl_map = lambda bi, g, i: (0, g)
        vtl_map = lambda bi, g, i: (0, g, 0, 0, 0)
    kl_spec = pl.BlockSpec((kl_rows, LANES), kl_map)
    vtl_spec = pl.BlockSpec((None, None, chunks, VT_ROWS, ck), vtl_map)
    kc_spec = pl.BlockSpec((c, LANES), lambda bi, g, i: (ctx_block0 + bi, g))
    vtc_spec = pl.BlockSpec((None, None, VT_ROWS, c), lambda bi, g, i: (bi, g, 0, 0))
    args = [q, k2, vt_lat, k2, vt_ctx]
    in_specs = [q_spec, kl_spec, vtl_spec, kc_spec, vtc_spec]
    aliases = {}
    if o_prev is not None:
        args.append(o_prev)
        in_specs.append(pl.BlockSpec(memory_space=pl.ANY))
        aliases = {len(args) - 1: 0}
    return pl.pallas_call(
        functools.partial(_flash_t_kernel, n_lat_chunks=n_lat_chunks, ck=ck),
        grid=(b, ATTN_KV_HEADS, nq),
        in_specs=in_specs,
        out_specs=q_spec,
        out_shape=jax.ShapeDtypeStruct(q.shape, BF16),
        input_output_aliases=aliases,
        scratch_shapes=(
            [pltpu.VMEM((ck, tq) if n_lat_chunks else (SUBLANES, LANES), F32)] * 8
            + [pltpu.VMEM((1, tq), F32)] * 4 + [pltpu.VMEM((VT_ROWS, tq), F32)] * 4),
        compiler_params=_params(3),
        name="flash_lat" if latent_queries else "flash_ctx",
    )(*args)


def _rope_tables(seq, tm):
    rows = seq // GRID_W
    row = jnp.repeat(jnp.arange(rows, dtype=F32), GRID_W)
    col = jnp.tile(jnp.arange(GRID_W, dtype=F32), rows)
    axis_dim = ATTN_HEAD_DIM // 2
    inv_freq = ROPE_THETA ** (-jnp.arange(0, axis_dim, 2, dtype=F32) / axis_dim)
    ang = jnp.concatenate([row[:, None] * inv_freq, col[:, None] * inv_freq], axis=-1)
    cos = jnp.repeat(jnp.cos(ang), 2, axis=-1)
    sign = jnp.tile(jnp.array([-1.0, 1.0], F32), ATTN_HEAD_DIM // 2)
    sin = jnp.repeat(jnp.sin(ang), 2, axis=-1) * sign
    reps = LANES // ATTN_HEAD_DIM
    cos = jnp.concatenate([jnp.tile(cos, (1, reps)), jnp.ones((tm, LANES), F32)], axis=0)
    sin = jnp.concatenate([jnp.tile(sin, (1, reps)), jnp.zeros((tm, LANES), F32)], axis=0)
    return cos, sin


def _head_mean_matrix(n, head_dim):
    idx = jnp.arange(n) // head_dim
    return jnp.where(idx[:, None] == idx[None, :], 1.0 / head_dim, 0.0).astype(BF16)


def kernel(x, c, ctx, c_ctx, ada_w, ada_b, ln_mix_g, ln_mix_b, ln_ffn_g, ln_ffn_b, ffn_w1, ffn_w3, ffn_w2, conv_pw1_w, conv_pw1_b, conv_dw_w, conv_dw_b, conv_norm_g, conv_norm_b, conv_pw2_w, conv_pw2_b, ssm_in_w, ssm_conv_w, ssm_conv_b, ssm_a_log, ssm_dt_bias, ssm_d, ssm_norm_g, ssm_out_w, attn_qkv_w, attn_q_norm, attn_k_norm, attn_out_w):
    batch, seq, d = x.shape
    ctx_len = ctx.shape[1]
    geom = _Geom(batch, seq, ctx_len, d)
    tm = math.gcd(512, math.gcd(seq, batch * ctx_len))
    tmc = math.gcd(256, math.gcd(seq, ctx_len))
    assert batch < MOD_ROWS and seq % GRID_W == 0 and ctx_len % SSM_CHUNK == 0

    cond = jnp.zeros((MOD_ROWS, d), F32).at[:batch].set(c).at[batch].set(c_ctx)
    mods = _modulation_table(cond, ada_w, ada_b)

    h = jnp.concatenate([x.reshape(batch * seq, d), ctx.reshape(batch * ctx_len, d)], axis=0)
    vec = lambda a: a.reshape(1, -1)

    for i in range(DEPTH):
        last = i == DEPTH - 1
        kind, j = i % N_MIXERS, i // N_MIXERS
        n_rows = geom.rows_lat if (last and kind == 0) else h.shape[0]
        n_rows = min(n_rows, h.shape[0])
        g_mix, b_mix = vec(ln_mix_g[i]), vec(ln_mix_b[i])

        if kind == 0:
            h = _conv_layer(geom, i, h, mods, conv_pw1_w[j].astype(BF16), vec(conv_pw1_b[j]),
                            conv_dw_w[j], vec(conv_dw_b[j]), vec(conv_norm_g[j]),
                            vec(conv_norm_b[j]), conv_pw2_w[j].astype(BF16), vec(conv_pw2_b[j]),
                            g_mix, b_mix, n_rows, tmc)
        elif kind == 1:
            in_w = ssm_in_w[j]
            d_inner = ssm_norm_g.shape[1]
            n_xbc = ssm_conv_w.shape[2]
            n_heads2 = ssm_a_log.shape[1]
            wz = in_w[:, :d_inner].astype(BF16)
            wx = in_w[:, d_inner:d_inner + n_xbc].astype(BF16)
            wd = jnp.pad(in_w[:, d_inner + n_xbc:], ((0, 0), (0, LANES - n_heads2)))
            wd_hi = wd.astype(BF16)
            wd_lo = (wd - wd_hi.astype(F32)).astype(BF16)
            dt_bias = jnp.pad(ssm_dt_bias[j], (0, LANES - n_heads2)).reshape(1, LANES)
            ssm_args = (geom, i, h, mods, wz, wx, wd_hi, wd_lo, dt_bias, ssm_conv_w[j],
                        vec(ssm_conv_b[j]))
            outs = _ssm_in(*ssm_args, tm, 0, geom.rows_lat // tm)
            z, xbc, dt = _ssm_in(*ssm_args, tmc, geom.rows_lat, (n_rows - geom.rows_lat) // tmc,
                                 prev_out=outs)
            a_log = jnp.pad(ssm_a_log[j], (0, LANES - n_heads2), constant_values=NEG_BIG)
            yf, yb = _ssd(geom, xbc, dt, dt.T, a_log.reshape(1, LANES), a_log.reshape(LANES, 1),
                          d_inner, True)
            d_skip = jnp.repeat(ssm_d[j], SSM_HEAD_DIM).reshape(1, d_inner)
            h = _ssm_out(geom, i, yf, yb, xbc, z, d_skip, vec(ssm_norm_g[j]),
                         ssm_out_w[j].astype(BF16), h, mods, g_mix, b_mix, n_rows, tm)
        else:
            w = attn_qkv_w[j]
            hd = ATTN_HEAD_DIM
            n_q = attn_out_w.shape[1]
            n_kv = (w.shape[1] - n_q) // 2
            wq = w[:, :n_q].astype(BF16)
            wk = w[:, n_q:n_q + n_kv].reshape(d, ATTN_KV_HEADS, 1, hd)
            wk2 = jnp.broadcast_to(wk, (d, ATTN_KV_HEADS, 2, hd)).reshape(d, 2 * n_kv)
            wv = w[:, n_q + n_kv:].reshape(d, ATTN_KV_HEADS, 1, hd)
            wv2 = jnp.concatenate([wv, jnp.zeros_like(wv)], axis=2).reshape(d, 2 * n_kv)
            gq = jnp.tile(attn_q_norm[j], n_q // hd).reshape(1, n_q)
            gk = jnp.tile(attn_k_norm[j], 2 * n_kv // hd).reshape(1, 2 * n_kv)
            cos_t, sin_t = _rope_tables(seq, tm)
            q, k2, v2 = _qkv(geom, i, h, mods, wq, wk2.astype(BF16), wv2.astype(BF16),
                             _head_mean_matrix(MXU_COLS, hd), _head_mean_matrix(MXU_COLS, hd),
                             gq, gk, cos_t, sin_t, n_rows, tm)
            tq = math.gcd(512, seq)
            ck = math.gcd(1024, seq // 2)
            o = _flash(geom, q, k2, v2, tq, ck, True)
            o = _flash(geom, q, k2, v2, ctx_len, ck, False, o_prev=o)
            h = _proj_res(geom, i, o, attn_out_w[j].astype(BF16), h, mods, g_mix, b_mix,
                          n_rows, tm)

        h = _ffn(geom, i, h, mods, ffn_w1[i].astype(BF16), ffn_w3[i].astype(BF16),
                 ffn_w2[i].astype(BF16), vec(ln_ffn_g[i]), vec(ln_ffn_b[i]), n_rows, tm)

    return h[:geom.rows_lat].reshape(batch, seq, d)
```

```python
import functools
import math

import jax
import jax.numpy as jnp
from jax import lax
from jax.experimental import pallas as pl
from jax.experimental.pallas import tpu as pltpu

F32 = jnp.float32
BF16 = jnp.bfloat16

DEPTH = 4
GRID_W = 64
N_MIXERS = 3
DEEPNORM_ALPHA = (2 * DEPTH) ** 0.25
LN_EPS = 1e-5
RMS_EPS = 1e-6
CONV_WIDTH = 31
SSM_HEAD_DIM = 64
SSM_GROUPS = 8
SSM_STATE = 128
SSM_CONV_WIDTH = 5
SSM_CHUNK = 128
ATTN_HEAD_DIM = 64
ATTN_KV_HEADS = 4
ROPE_THETA = 10000.0

LANES = 128
SUBLANES = 8
BF16_ROWS = 16
VMEM_LIMIT = 56 * 1024 * 1024

MOD_ROWS = 8
NEG_BIG = -1e30


def _dot(a, b):
    return jnp.dot(a, b, preferred_element_type=F32)


def _dot_nt(a, b):
    return lax.dot_general(a, b, (((1,), (1,)), ((), ())), preferred_element_type=F32)


def _sigmoid(x):
    return 1.0 / (1.0 + jnp.exp(-x))


def _silu(x):
    return x * _sigmoid(x)


def _layer_norm(z, g, b):
    mu = jnp.mean(z, axis=-1, keepdims=True)
    d = z - mu
    var = jnp.mean(d * d, axis=-1, keepdims=True)
    return d * lax.rsqrt(var + LN_EPS) * g + b


def _split3(a):
    a1 = a.astype(BF16)
    r1 = a - a1.astype(F32)
    a2 = r1.astype(BF16)
    a3 = (r1 - a2.astype(F32)).astype(BF16)
    return a1, a2, a3


def _const_spec(shape):
    nd = len(shape)
    return pl.BlockSpec(shape, lambda *_: (0,) * nd, pipeline_mode=pl.Buffered(1))


def _params(n_grid_axes, sem=None):
    return pltpu.CompilerParams(
        dimension_semantics=sem or ("parallel",) * n_grid_axes,
        vmem_limit_bytes=VMEM_LIMIT)


class _Geom:
    def __init__(self, batch, seq, ctx_len, d_model):
        self.B, self.L, self.C, self.D = batch, seq, ctx_len, d_model
        self.rows_lat = batch * seq
        self.rows = batch * (seq + ctx_len)

    def mod_spec(self, layer, which, tm, tile0=0):
        n_lat_tiles = self.rows_lat // tm
        tiles_per_seq = self.L // tm
        b = self.B

        def imap(t, *_):
            t = t + tile0
            row = jnp.where(t < n_lat_tiles, t // tiles_per_seq, b)
            return ((layer * MOD_ROWS + row) * 6 + which, 0, 0)

        return pl.BlockSpec((None, 1, self.D), imap)


def _mod_kernel(cond_ref, w_ref, b_ref, o_ref):
    s = _silu(cond_ref[...])
    w = w_ref[...]
    s_hi = s.astype(BF16)
    s_lo = (s - s_hi.astype(F32)).astype(BF16)
    w_hi = w.astype(BF16)
    w_lo = (w - w_hi.astype(F32)).astype(BF16)
    acc = _dot(s_hi, w_hi) + _dot(s_lo, w_hi) + _dot(s_hi, w_lo)
    o_ref[...] = acc + b_ref[...]


def _modulation_table(cond, ada_w, ada_b):
    depth, d, n = ada_w.shape
    tn = n // 4
    out = pl.pallas_call(
        _mod_kernel,
        grid=(depth, n // tn),
        in_specs=[
            pl.BlockSpec((MOD_ROWS, d), lambda l, j: (0, 0)),
            pl.BlockSpec((None, d, tn), lambda l, j: (l, 0, j)),
            pl.BlockSpec((None, 1, tn), lambda l, j: (l, 0, j)),
        ],
        out_specs=pl.BlockSpec((None, MOD_ROWS, tn), lambda l, j: (l, 0, j)),
        out_shape=jax.ShapeDtypeStruct((depth, MOD_ROWS, n), F32),
        compiler_params=_params(2),
        name="mod_table",
    )(cond, ada_w, ada_b.reshape(depth, 1, n))
    return out.reshape(depth * MOD_ROWS * 6, 1, d)


def _ffn_kernel(h_ref, sh_ref, sc_ref, gt_ref, w1_ref, w3_ref, w2_ref, g_ref, b_ref, o_ref,
                *, chunk):
    h = h_ref[...]
    u = (h * (1.0 + sc_ref[...]) + sh_ref[...]).astype(BF16)
    hidden = w1_ref.shape[1]
    acc = jnp.zeros(h.shape, F32)
    for c in range(0, hidden, chunk):
        a = _dot(u, w1_ref[:, c:c + chunk])
        b = _dot(u, w3_ref[:, c:c + chunk])
        g = (_silu(a) * b).astype(BF16)
        acc = acc + _dot(g, w2_ref[c:c + chunk, :])
    z = DEEPNORM_ALPHA * h + gt_ref[...] * acc
    o_ref[...] = _layer_norm(z, g_ref[...], b_ref[...])


def _ffn(geom, layer, h, mods, w1, w3, w2, ln_g, ln_b, n_rows, tm):
    d = geom.D
    hidden = w1.shape[1]
    row = pl.BlockSpec((tm, d), lambda t: (t, 0))
    return pl.pallas_call(
        functools.partial(_ffn_kernel, chunk=2 * LANES),
        grid=(n_rows // tm,),
        in_specs=[row, geom.mod_spec(layer, 3, tm), geom.mod_spec(layer, 4, tm),
                  geom.mod_spec(layer, 5, tm),
                  _const_spec((d, hidden)), _const_spec((d, hidden)), _const_spec((hidden, d)),
                  _const_spec((1, d)), _const_spec((1, d))],
        out_specs=row,
        out_shape=jax.ShapeDtypeStruct((n_rows, d), F32),
        compiler_params=_params(1),
        name="ffn",
    )(h, mods, mods, mods, w1, w3, w2, ln_g, ln_b)


def _proj_res_kernel(x_ref, w_ref, h_ref, gt_ref, g_ref, b_ref, o_ref):
    y = _dot(x_ref[...], w_ref[...])
    z = DEEPNORM_ALPHA * h_ref[...] + gt_ref[...] * y
    o_ref[...] = _layer_norm(z, g_ref[...], b_ref[...])


def _proj_res(geom, layer, x, w, h, mods, ln_g, ln_b, n_rows, tm):
    d = geom.D
    k = x.shape[1]
    return pl.pallas_call(
        _proj_res_kernel,
        grid=(n_rows // tm,),
        in_specs=[pl.BlockSpec((tm, k), lambda t: (t, 0)), _const_spec((k, d)),
                  pl.BlockSpec((tm, d), lambda t: (t, 0)), geom.mod_spec(layer, 2, tm),
                  _const_spec((1, d)), _const_spec((1, d))],
        out_specs=pl.BlockSpec((tm, d), lambda t: (t, 0)),
        out_shape=jax.ShapeDtypeStruct((n_rows, d), F32),
        compiler_params=_params(1),
        name="proj_res",
    )(x, w, h, mods, ln_g, ln_b)


def _seq_edges(t, tm, geom):
    n_lat_tiles = geom.rows_lat // tm
    lat_tps = geom.L // tm
    ctx_tps = max(geom.C // tm, 1)
    in_lat = t < n_lat_tiles
    pos = jnp.where(in_lat, t % lat_tps, (t - n_lat_tiles) % ctx_tps)
    tps = jnp.where(in_lat, lat_tps, ctx_tps)
    return pos == 0, pos == tps - 1


def _store_window(win_ref, cols, value, halo, tm, is_start, is_end):
    win_ref[:, cols] = value
    win_ref[0:halo, cols] = jnp.where(is_start, 0.0, value[0:halo])
    win_ref[halo + tm:, cols] = jnp.where(is_end, 0.0, value[halo + tm:])


def _halo_specs(tm, halo, n_cols, n_rows_total, tile0=0):
    per = tm // halo
    last = n_rows_total // halo - 1
    tile = lambda t: jnp.clip(t + tile0, 0, n_rows_total // tm - 1)
    prev = pl.BlockSpec((halo, n_cols), lambda t: (jnp.maximum(tile(t) * per - 1, 0), 0))
    cur = pl.BlockSpec((tm, n_cols), lambda t: (tile(t), 0))
    nxt = pl.BlockSpec((halo, n_cols), lambda t: (jnp.minimum((tile(t) + 1) * per, last), 0))
    return prev, cur, nxt


def _depthwise(win_ref, w_ref, width, first, tm, cols, emit, row_block=64):
    phases = {}
    for k in range(width):
        a, b = divmod(first + k, SUBLANES)
        phases.setdefault(b, []).append((a, k))
    for c in range(cols.start, cols.stop, LANES):
        for r in range(0, tm, row_block):
            acc = None
            for b, taps in sorted(phases.items()):
                z = None
                for a, k in taps:
                    lo = r + SUBLANES * a
                    term = (win_ref[lo:lo + row_block + SUBLANES, c:c + LANES]
                            * w_ref[k:k + 1, c:c + LANES])
                    z = term if z is None else z + term
                zb = z[b:b + row_block]
                acc = zb if acc is None else acc + zb
            emit(r, c, acc)


MXU_COLS = 2 * LANES


CONV_HALO = 16


def _conv_layer_kernel(*refs, geom, tm, split_input):
    n_in = 6 if split_input else 3
    (sh_ref, sc_ref, gt_ref, wa_ref, wg_ref, ba_ref, bg_ref, dww_ref, dwb_ref, ng_ref, nb_ref,
     w2_ref, b2_ref, g_ref, b_ref, o_ref, win_ref, cv_ref) = refs[n_in:]
    halo = CONV_HALO
    t = pl.program_id(0)
    is_start, is_end = _seq_edges(t, tm, geom)
    hwin = jnp.concatenate([r[...] for r in refs[0:3]], axis=0)
    h_tile = refs[1][...]
    if split_input:
        in_lat = t < geom.rows_lat // tm
        hwin = jnp.where(in_lat, hwin, jnp.concatenate([r[...] for r in refs[3:6]], axis=0))
        h_tile = jnp.where(in_lat, h_tile, refs[4][...])
    u = (hwin * (1.0 + sc_ref[...]) + sh_ref[...]).astype(BF16)

    def glu_block(cols):
        a = _dot(u, wa_ref[:, cols]) + ba_ref[:, cols]
        g = _dot(u, wg_ref[:, cols]) + bg_ref[:, cols]
        _store_window(win_ref, cols, a * _sigmoid(g), halo, tm, is_start, is_end)

    def emit(r, c, acc):
        cv_ref[r:r + acc.shape[0], c:c + LANES] = acc + dwb_ref[:, c:c + LANES]

    blocks = [slice(c, c + MXU_COLS) for c in range(0, geom.D, MXU_COLS)]
    glu_block(blocks[0])
    for j, cols in enumerate(blocks):
        if j + 1 < len(blocks):
            glu_block(blocks[j + 1])
        _depthwise(win_ref, dww_ref, CONV_WIDTH, halo - (CONV_WIDTH - 1) // 2, tm, cols, emit)
    v = _silu(_layer_norm(cv_ref[...], ng_ref[...], nb_ref[...])).astype(BF16)
    y = _dot(v, w2_ref[...]) + b2_ref[...]
    z = DEEPNORM_ALPHA * h_tile + gt_ref[...] * y
    o_ref[...] = _layer_norm(z, g_ref[...], b_ref[...])


def _conv_layer(geom, layer, h, mods, w1, b1, dw_w, dw_b, n_g, n_b, w2, b2, ln_g, ln_b,
                n_rows, tm):
    d = geom.D
    split_input = isinstance(h, tuple)
    if split_input:
        streams = [h[0]] * 3 + [h[1]] * 3
        stream_specs = (list(_halo_specs(tm, CONV_HALO, d, h[0].shape[0]))
                        + list(_halo_specs(tm, CONV_HALO, d, h[1].shape[0],
                                           tile0=-(h[0].shape[0] // tm))))
    else:
        streams = [h] * 3
        stream_specs = list(_halo_specs(tm, CONV_HALO, d, h.shape[0]))
    half = lambda j: pl.BlockSpec((d, d), lambda t: (0, j), pipeline_mode=pl.Buffered(1))
    bhalf = lambda j: pl.BlockSpec((1, d), lambda t: (0, j), pipeline_mode=pl.Buffered(1))
    vec = _const_spec((1, d))
    return pl.pallas_call(
        functools.partial(_conv_layer_kernel, geom=geom, tm=tm, split_input=split_input),
        grid=(n_rows // tm,),
        in_specs=stream_specs + [
            geom.mod_spec(layer, 0, tm), geom.mod_spec(layer, 1, tm),
            geom.mod_spec(layer, 2, tm), half(0), half(1), bhalf(0), bhalf(1),
            _const_spec((CONV_WIDTH, d)), vec, vec, vec, _const_spec((d, d)), vec, vec, vec],
        out_specs=pl.BlockSpec((tm, d), lambda t: (t, 0)),
        out_shape=jax.ShapeDtypeStruct((n_rows, d), F32),
        scratch_shapes=[pltpu.VMEM((tm + 2 * CONV_HALO, d), F32), pltpu.VMEM((tm, d), F32)],
        compiler_params=_params(1),
        name="conv_layer",
    )(*streams, mods, mods, mods, w1, w1, b1, b1, dw_w, dw_b, n_g, n_b, w2, b2, ln_g, ln_b)


SSM_HALO = 8


def _ssm_in_kernel(hp_ref, hc_ref, hn_ref, sh_ref, sc_ref, wz_ref, wx_ref, wdh_ref, wdl_ref,
                   dtb_ref, cw_ref, cb_ref, *rest, geom, tm, tile0, col_chunk):
    z_ref, x_ref, dt_ref, dtt_ref, win_ref = rest[-5:]
    halo = SSM_HALO
    is_start, is_end = _seq_edges(pl.program_id(0) + tile0, tm, geom)
    hwin = jnp.concatenate([hp_ref[...], hc_ref[...], hn_ref[...]], axis=0)
    uf = hwin * (1.0 + sc_ref[...]) + sh_ref[...]
    u = uf.astype(BF16)
    u_lo = (uf - u.astype(F32)).astype(BF16)
    own = slice(halo, halo + tm)

    def project_x(cols):
        _store_window(win_ref, cols, _dot(u, wx_ref[:, cols]), halo, tm, is_start, is_end)

    def emit(r, c, acc):
        col = slice(c, c + LANES)
        x_ref[r:r + acc.shape[0], col] = _silu(acc + cb_ref[:, col]).astype(BF16)

    def conv_x(cols):
        _depthwise(win_ref, cw_ref, SSM_CONV_WIDTH, halo - (SSM_CONV_WIDTH - 1) // 2, tm, cols,
                   emit)

    def project_z(cols):
        z_ref[:, cols] = _dot(u, wz_ref[:, cols])[own].astype(BF16)

    def project_dt():
        d = (_dot(u, wdh_ref[...]) + _dot(u_lo, wdh_ref[...]) + _dot(u, wdl_ref[...]))[own]
        d = d + dtb_ref[...]
        dt = jnp.maximum(d, 0.0) + jnp.log1p(jnp.exp(-jnp.abs(d)))
        dt_ref[...] = dt
        dtt_ref[...] = dt.T

    x_blocks = [slice(c, c + col_chunk) for c in range(0, wx_ref.shape[1], col_chunk)]
    z_blocks = [slice(c, c + col_chunk) for c in range(0, wz_ref.shape[1], col_chunk)]
    extra = [functools.partial(project_z, cols) for cols in z_blocks] + [project_dt]
    per_conv = -(-len(extra) // len(x_blocks))
    project_x(x_blocks[0])
    for j, cols in enumerate(x_blocks):
        if j + 1 < len(x_blocks):
            project_x(x_blocks[j + 1])
        for f in extra[j * per_conv:(j + 1) * per_conv]:
            f()
        conv_x(cols)


def _ssm_in(geom, layer, h, mods, wz, wx, wd_hi, wd_lo, dt_bias, conv_w, conv_b, tm, row0,
            n_tiles, prev_out=None):
    d = geom.D
    nz, nx = wz.shape[1], wx.shape[1]
    n_rows = h.shape[0]
    tile0 = row0 // tm
    col_chunk = MXU_COLS
    prev, cur, nxt = _halo_specs(tm, SSM_HALO, d, n_rows, tile0)
    args = [h, h, h, mods, mods, wz, wx, wd_hi, wd_lo, dt_bias, conv_w, conv_b]
    in_specs = [prev, cur, nxt, geom.mod_spec(layer, 0, tm, tile0),
                geom.mod_spec(layer, 1, tm, tile0),
                _const_spec((d, nz)), _const_spec((d, nx)),
                _const_spec((d, LANES)), _const_spec((d, LANES)), _const_spec((1, LANES)),
                _const_spec((SSM_CONV_WIDTH, nx)), _const_spec((1, nx))]
    aliases = {}
    if prev_out is not None:
        for o in prev_out:
            aliases[len(args)] = len(aliases)
            args.append(o)
            in_specs.append(pl.BlockSpec(memory_space=pl.ANY))
    return pl.pallas_call(
        functools.partial(_ssm_in_kernel, geom=geom, tm=tm, tile0=tile0, col_chunk=col_chunk),
        grid=(n_tiles,),
        in_specs=in_specs,
        out_specs=[pl.BlockSpec((tm, nz), lambda t: (t + tile0, 0)),
                   pl.BlockSpec((tm, nx), lambda t: (t + tile0, 0)),
                   pl.BlockSpec((tm, LANES), lambda t: (t + tile0, 0)),
                   pl.BlockSpec((LANES, tm), lambda t: (0, t + tile0))],
        out_shape=[jax.ShapeDtypeStruct((n_rows, nz), BF16),
                   jax.ShapeDtypeStruct((n_rows, nx), BF16),
                   jax.ShapeDtypeStruct((n_rows, LANES), F32),
                   jax.ShapeDtypeStruct((LANES, n_rows), F32)],
        input_output_aliases=aliases,
        scratch_shapes=[pltpu.VMEM((tm + 2 * SSM_HALO, nx), F32)],
        compiler_params=_params(1),
        name="ssm_in",
    )(*args)


def _ssd_direction(x_ref, b_ref, c_ref, dt_ref, dtt_ref, alog_row_ref, alog_col_ref, st_ref,
                   y_ref, *, backward):
    q = SSM_CHUNK
    n_state = SSM_STATE
    hpg = 4
    ii = lax.broadcasted_iota(jnp.int32, (q, q), 0)
    jj = lax.broadcasted_iota(jnp.int32, (q, q), 1)
    mask = (ii <= jj) if backward else (ii >= jj)
    tri = mask.astype(BF16)
    tri_t = ((ii >= jj) if backward else (ii <= jj)).astype(BF16)

    dt = dt_ref[...]
    dtt = dtt_ref[...]
    a_col = dt * (-jnp.exp(alog_row_ref[...]))
    a_row = dtt * (-jnp.exp(alog_col_ref[...]))
    cum_col = sum(_dot(tri, p) for p in _split3(a_col))
    cum_row = sum(_dot(p, tri_t) for p in _split3(a_row))
    last = 0 if backward else q - 1
    cum_last_col = cum_row[:, last:last + 1]
    w_row = jnp.exp(cum_last_col - cum_row) * dtt
    e_last_row = jnp.exp(cum_col[last:last + 1, :])

    lane = lax.broadcasted_iota(jnp.int32, (q, LANES), 1)
    low_half = lane < SSM_HEAD_DIM
    base = 32 * (1 if backward else 0)

    def group(g):
        bg = b_ref[:, g * n_state:(g + 1) * n_state]
        cg = c_ref[:, g * n_state:(g + 1) * n_state]
        cb = _dot_nt(cg, bg)
        bt = bg.astype(F32).T
        cg32 = cg.astype(F32)
        yield
        for pair in range(hpg // 2):
            slot = g * (hpg // 2) + pair
            xp = x_ref[:, slot * LANES:(slot + 1) * LANES]
            hs = st_ref[slot]
            rhs = jnp.concatenate([xp, hs.astype(BF16)], axis=0)
            ys, ss, els = [], [], []
            for r in range(2):
                hl = base + g * hpg + pair * 2 + r
                cc = jnp.broadcast_to(cum_col[:, hl:hl + 1], (q, LANES))
                cr = cum_row[hl:hl + 1, :]
                seg = jnp.exp(jnp.where(mask, cc - cr, NEG_BIG))
                m = (cb * seg * dtt[hl:hl + 1, :]).astype(BF16)
                ce = (cg32 * jnp.exp(cc)).astype(BF16)
                ys.append(_dot(jnp.concatenate([m, ce], axis=1), rhs))
                bts = (bt * w_row[hl:hl + 1, :]).astype(BF16)
                ss.append(_dot(bts, xp))
                els.append(e_last_row[:, hl:hl + 1])
            y_ref[:, slot * LANES:(slot + 1) * LANES] = jnp.where(
                low_half, ys[0], ys[1]).astype(y_ref.dtype)
            low_n = low_half[:n_state]
            decay = jnp.where(low_n, els[0], els[1])
            st_ref[slot] = hs * decay + jnp.where(low_n, ss[0], ss[1])
            yield

    return group


def _ssd_kernel(xf_ref, bf_ref, cf_ref, dtf_ref, dttf_ref,
                xb_ref, bb_ref, cb_ref, dtb_ref, dttb_ref,
                alog_row_ref, alog_col_ref, yf_ref, yb_ref, stf_ref, stb_ref):
    @pl.when(pl.program_id(1) == 0)
    def _():
        stf_ref[...] = jnp.zeros(stf_ref.shape, F32)
        stb_ref[...] = jnp.zeros(stb_ref.shape, F32)

    fwd = _ssd_direction(xf_ref, bf_ref, cf_ref, dtf_ref, dttf_ref, alog_row_ref, alog_col_ref,
                         stf_ref, yf_ref, backward=False)
    bwd = _ssd_direction(xb_ref, bb_ref, cb_ref, dtb_ref, dttb_ref, alog_row_ref, alog_col_ref,
                         stb_ref, yb_ref, backward=True)
    for g in range(0, SSM_GROUPS, 2):
        for _ in zip(fwd(g), bwd(g), fwd(g + 1), bwd(g + 1)):
            pass


def _ssd(geom, xbc, dt, dtt, alog_row, alog_col, d_inner, with_ctx_out):
    q = SSM_CHUNK
    n_bc = SSM_GROUPS * SSM_STATE
    lat_chunks = geom.L // q
    ctx_chunks = geom.C // q
    lat_total = geom.rows_lat // q
    steps = lat_chunks + ctx_chunks

    def fwd_chunk(b, s):
        return jnp.where(s < ctx_chunks, lat_total + b * ctx_chunks + s,
                         b * lat_chunks + (s - ctx_chunks))

    def bwd_chunk(b, s):
        return jnp.where(s < ctx_chunks, lat_total + b * ctx_chunks + (ctx_chunks - 1 - s),
                         b * lat_chunks + (lat_chunks - 1 - (s - ctx_chunks)))

    def specs(chunk):
        return [pl.BlockSpec((q, d_inner), lambda b, s: (chunk(b, s), 0)),
                pl.BlockSpec((q, n_bc), lambda b, s: (chunk(b, s), d_inner // n_bc)),
                pl.BlockSpec((q, n_bc), lambda b, s: (chunk(b, s), d_inner // n_bc + 1)),
                pl.BlockSpec((q, LANES), lambda b, s: (chunk(b, s), 0)),
                pl.BlockSpec((LANES, q), lambda b, s: (0, chunk(b, s)))]

    n_rows = geom.rows
    y_shape = jax.ShapeDtypeStruct((n_rows, d_inner), BF16)
    state = pltpu.VMEM((SSM_GROUPS * 2, SSM_STATE, LANES), F32)
    return pl.pallas_call(
        _ssd_kernel,
        grid=(geom.B, steps),
        in_specs=specs(fwd_chunk) + specs(bwd_chunk)
        + [pl.BlockSpec((1, LANES), lambda b, s: (0, 0)),
           pl.BlockSpec((LANES, 1), lambda b, s: (0, 0))],
        out_specs=[pl.BlockSpec((q, d_inner), lambda b, s: (fwd_chunk(b, s), 0)),
                   pl.BlockSpec((q, d_inner), lambda b, s: (bwd_chunk(b, s), 0))],
        out_shape=[y_shape, y_shape],
        scratch_shapes=[state, state],
        compiler_params=_params(2, ("parallel", "arbitrary")),
        name="ssd_scan",
    )(xbc, xbc, xbc, dt, dtt, xbc, xbc, xbc, dt, dtt, alog_row, alog_col)


def _ssm_out_kernel(yf_ref, yb_ref, x_ref, z_ref, dsk_ref, ng_ref, w_ref, h_ref, gt_ref,
                    g_ref, b_ref, o_ref):
    d_inner = yf_ref.shape[1]
    gw = d_inner // SSM_GROUPS
    acc = jnp.zeros(o_ref.shape, F32)
    for c in range(0, d_inner, gw):
        sl = slice(c, c + gw)
        y = (yf_ref[:, sl].astype(F32) + yb_ref[:, sl].astype(F32)
             + dsk_ref[:, sl] * x_ref[:, sl].astype(F32))
        gt = y * _silu(z_ref[:, sl].astype(F32))
        ms = jnp.mean(gt * gt, axis=-1, keepdims=True)
        gn = (gt * lax.rsqrt(ms + RMS_EPS) * ng_ref[:, sl]).astype(BF16)
        acc = acc + _dot(gn, w_ref[sl, :])
    zz = DEEPNORM_ALPHA * h_ref[...] + gt_ref[...] * acc
    o_ref[...] = _layer_norm(zz, g_ref[...], b_ref[...])


def _ssm_out(geom, layer, yf, yb, xbc, z, d_skip, norm_g, out_w, h, mods, ln_g, ln_b,
             n_rows, tm):
    d = geom.D
    d_inner = z.shape[1]
    wide = pl.BlockSpec((tm, d_inner), lambda t: (t, 0))
    row = pl.BlockSpec((tm, d), lambda t: (t, 0))
    return pl.pallas_call(
        _ssm_out_kernel,
        grid=(n_rows // tm,),
        in_specs=[wide, wide, wide, wide, _const_spec((1, d_inner)), _const_spec((1, d_inner)),
                  _const_spec((d_inner, d)), row, geom.mod_spec(layer, 2, tm),
                  _const_spec((1, d)), _const_spec((1, d))],
        out_specs=row,
        out_shape=jax.ShapeDtypeStruct((n_rows, d), F32),
        compiler_params=_params(1),
        name="ssm_out",
    )(yf, yb, xbc, z, d_skip, norm_g, out_w, h, mods, ln_g, ln_b)


def _qkv_kernel(h_ref, sh_ref, sc_ref, wq_ref, wk_ref, wv_ref, pq_ref, pk_ref, gq_ref, gk_ref,
                cos_ref, sin_ref, q_ref, k_ref, v_ref):
    u = (h_ref[...] * (1.0 + sc_ref[...]) + sh_ref[...]).astype(BF16)
    cos = cos_ref[...]
    sin = sin_ref[...]

    def norm_rope(x, p_ref, g_ref):
        n = x.shape[1]
        sq = (x * x).astype(BF16)
        ms = jnp.concatenate([_dot(sq[:, c:c + MXU_COLS], p_ref[...])
                              for c in range(0, n, MXU_COLS)], axis=1)
        xn = x * lax.rsqrt(ms + RMS_EPS) * g_ref[...]
        lane = lax.broadcasted_iota(jnp.int32, xn.shape, 1)
        partner = jnp.where(lane % 2 == 0, pltpu.roll(xn, n - 1, axis=1),
                            pltpu.roll(xn, 1, axis=1))
        reps = n // cos.shape[1]
        return xn * jnp.tile(cos, (1, reps)) + partner * jnp.tile(sin, (1, reps))

    q = norm_rope(_dot(u, wq_ref[...]), pq_ref, gq_ref)
    q_ref[...] = (q * (ATTN_HEAD_DIM ** -0.5 * math.log2(math.e))).astype(BF16)
    k_ref[...] = norm_rope(_dot(u, wk_ref[...]), pk_ref, gk_ref).astype(BF16)
    v = _dot(u, wv_ref[...])
    lane = lax.broadcasted_iota(jnp.int32, v.shape, 1)
    v_ref[...] = jnp.where(lane % LANES < ATTN_HEAD_DIM, v, 1.0).astype(BF16)


def _qkv(geom, layer, h, mods, wq, wk2, wv, pq, pk, gq, gk, cos_t, sin_t, n_rows, tm):
    d = geom.D
    nq, nk, nv = wq.shape[1], wk2.shape[1], wv.shape[1]
    n_lat_tiles = geom.rows_lat // tm
    tps = geom.L // tm
    table = pl.BlockSpec((tm, LANES), lambda t: (jnp.where(t < n_lat_tiles, t % tps, tps), 0))
    return pl.pallas_call(
        _qkv_kernel,
        grid=(n_rows // tm,),
        in_specs=[pl.BlockSpec((tm, d), lambda t: (t, 0)),
                  geom.mod_spec(layer, 0, tm), geom.mod_spec(layer, 1, tm),
                  _const_spec((d, nq)), _const_spec((d, nk)), _const_spec((d, nv)),
                  _const_spec((MXU_COLS, MXU_COLS)), _const_spec((MXU_COLS, MXU_COLS)),
                  _const_spec((1, nq)), _const_spec((1, nk)), table, table],
        out_specs=[pl.BlockSpec((tm, nq), lambda t: (t, 0)),
                   pl.BlockSpec((tm, nk), lambda t: (t, 0)),
                   pl.BlockSpec((tm, nv), lambda t: (t, 0))],
        out_shape=[jax.ShapeDtypeStruct((n_rows, nq), BF16),
                   jax.ShapeDtypeStruct((n_rows, nk), BF16),
                   jax.ShapeDtypeStruct((n_rows, nv), BF16)],
        compiler_params=_params(1),
        name="attn_qkv",
    )(h, mods, mods, wq, wk2, wv, pq, pk, gq, gk, cos_t, sin_t)


def _flash_kernel(q_ref, kl_ref, vl_ref, kc_ref, vc_ref, *rest, n_lat_chunks, ck):
    n_heads = 4
    n_scratch = 4 * n_heads
    o_ref = rest[-n_scratch - 1]
    s_refs = (rest[-n_scratch:-3 * n_heads], rest[-3 * n_heads:-2 * n_heads])
    m_refs = rest[-2 * n_heads:-n_heads]
    acc_refs = rest[-n_heads:]
    tq = q_ref.shape[0]
    lane = lax.broadcasted_iota(jnp.int32, (tq, LANES), 1)
    low_half = lane < ATTN_HEAD_DIM

    qms = []
    for pair in range(n_heads // 2):
        q2 = q_ref[:, pair * LANES:(pair + 1) * LANES].astype(F32)
        qms.append(jnp.where(low_half, q2, 0.0).astype(BF16))
        qms.append(jnp.where(low_half, 0.0, q2).astype(BF16))

    def update_head(h, s, vc):
        m = m_refs[h][...]
        row_max = jnp.max(s, axis=-1, keepdims=True)
        m_new = jnp.maximum(m, jnp.broadcast_to(row_max, m.shape))
        m_wide = jnp.concatenate([m_new] * (s.shape[1] // LANES), axis=1)
        p = jnp.exp2(s - m_wide).astype(BF16)
        acc_refs[h][...] = jnp.exp2(m - m_new) * acc_refs[h][...] + _dot(p, vc)
        m_refs[h][...] = m_new

    def lat_rows(chunk):
        if isinstance(chunk, int):
            return pl.ds(chunk * ck, ck)
        return pl.ds(pl.multiple_of(chunk * ck, ck), ck)

    def step(cur, nxt, next_chunk, cur_chunk):
        kc = kl_ref[lat_rows(next_chunk), :]
        vc = vl_ref[lat_rows(cur_chunk), :]
        for h in range(n_heads):
            s_refs[nxt][h][...] = _dot_nt(qms[h], kc)
            update_head(h, s_refs[cur][h][...], vc)

    for h in range(n_heads):
        m_refs[h][...] = jnp.full((tq, LANES), NEG_BIG, F32)
        acc_refs[h][...] = jnp.zeros((tq, LANES), F32)
    n_ctx = kc_ref.shape[0]
    if n_lat_chunks:
        for h in range(n_heads):
            s_refs[0][h][...] = _dot_nt(qms[h], kl_ref[lat_rows(0), :])

        def body(t, _):
            c0 = 2 * t
            step(0, 1, c0 + 1, c0)
            step(1, 0, c0 + 2, c0 + 1)
            return 0

        lax.fori_loop(0, n_lat_chunks // 2 - 1, body, 0)
        step(0, 1, n_lat_chunks - 1, n_lat_chunks - 2)
        vc = vl_ref[lat_rows(n_lat_chunks - 1), :]
        for h in range(n_heads):
            s_refs[0][h][:, :n_ctx] = _dot_nt(qms[h], kc_ref[...])
            update_head(h, s_refs[1][h][...], vc)
        for h in range(n_heads):
            update_head(h, s_refs[0][h][:, :n_ctx], vc_ref[...])
    else:
        for h in range(n_heads):
            update_head(h, _dot_nt(qms[h], kc_ref[...]), vc_ref[...])

    accs = [r[...] for r in acc_refs]
    outs = [acc * (1.0 / pltpu.roll(acc, ATTN_HEAD_DIM, axis=1)) for acc in accs]
    for pair in range(n_heads // 2):
        o_pair = jnp.where(low_half, outs[2 * pair],
                           pltpu.roll(outs[2 * pair + 1], ATTN_HEAD_DIM, axis=1))
        o_ref[:, pair * LANES:(pair + 1) * LANES] = o_pair.astype(BF16)


def _flash(geom, q, k2, v2, tq, ck, latent_queries, o_prev=None):
    b, l, c = geom.B, geom.L, geom.C
    gw = 4 * ATTN_HEAD_DIM
    ctx_block0 = geom.rows_lat // c
    if latent_queries:
        nq = l // tq
        q_spec = pl.BlockSpec((tq, gw), lambda bi, g, i: (bi * nq + i, g))
        kl_rows, n_lat_chunks = l, l // ck
        kl_map = lambda bi, g, i: (bi, g)
    else:
        nq = 1
        q_spec = pl.BlockSpec((tq, gw), lambda bi, g, i: (ctx_block0 + bi, g))
        kl_rows, n_lat_chunks = ck, 0
        kl_map = lambda bi, g, i: (0, g)
    kl_spec = pl.BlockSpec((kl_rows, LANES), kl_map)
    kc_spec = pl.BlockSpec((c, LANES), lambda bi, g, i: (ctx_block0 + bi, g))
    args = [q, k2, v2, k2, v2]
    in_specs = [q_spec, kl_spec, kl_spec, kc_spec, kc_spec]
    aliases = {}
    if o_prev is not None:
        args.append(o_prev)
        in_specs.append(pl.BlockSpec(memory_space=pl.ANY))
        aliases = {len(args) - 1: 0}
    return pl.pallas_call(
        functools.partial(_flash_kernel, n_lat_chunks=n_lat_chunks, ck=ck),
        grid=(b, ATTN_KV_HEADS, nq),
        in_specs=in_specs,
        out_specs=q_spec,
        out_shape=jax.ShapeDtypeStruct(q.shape, BF16),
        input_output_aliases=aliases,
        scratch_shapes=(
            [pltpu.VMEM((tq, ck) if n_lat_chunks else (SUBLANES, LANES), F32)] * 8
            + [pltpu.VMEM((tq, LANES), F32)] * 8),
        compiler_params=_params(3),
        name="flash_lat" if latent_queries else "flash_ctx",
    )(*args)


VT_ROWS = ATTN_HEAD_DIM + BF16_ROWS


def _flash_t_kernel(q_ref, kl_ref, vtl_ref, kc_ref, vtc_ref, *rest, n_lat_chunks, ck):
    n_heads = 4
    n_scratch = 4 * n_heads
    o_ref = rest[-n_scratch - 1]
    s_refs = (rest[-n_scratch:-3 * n_heads], rest[-3 * n_heads:-2 * n_heads])
    m_refs = rest[-2 * n_heads:-n_heads]
    acc_refs = rest[-n_heads:]
    tq = q_ref.shape[0]
    row = lax.broadcasted_iota(jnp.int32, (LANES, tq), 0)
    low_half = row < ATTN_HEAD_DIM

    qts = []
    for pair in range(n_heads // 2):
        q2t = q_ref[:, pair * LANES:(pair + 1) * LANES].astype(F32).T
        qts.append(jnp.where(low_half, q2t, 0.0).astype(BF16))
        qts.append(jnp.where(low_half, 0.0, q2t).astype(BF16))

    def update_head(h, st, vt):
        m = m_refs[h][...]
        m_new = jnp.maximum(m, jnp.max(st, axis=0, keepdims=True))
        p = jnp.exp2(st - m_new).astype(BF16)
        acc_refs[h][...] = jnp.exp2(m - m_new) * acc_refs[h][...] + _dot(vt, p)
        m_refs[h][...] = m_new

    def lat_rows(chunk):
        return pl.ds(pl.multiple_of(chunk * ck, ck), ck)

    def step(cur, nxt, next_chunk, cur_chunk):
        kc = kl_ref[lat_rows(next_chunk), :]
        vt = vtl_ref[cur_chunk]
        for h in range(n_heads):
            s_refs[nxt][h][...] = _dot(kc, qts[h])
            update_head(h, s_refs[cur][h][...], vt)

    for h in range(n_heads):
        m_refs[h][...] = jnp.full((1, tq), NEG_BIG, F32)
        acc_refs[h][...] = jnp.zeros((VT_ROWS, tq), F32)
    if n_lat_chunks:
        for h in range(n_heads):
            s_refs[0][h][...] = _dot(kl_ref[lat_rows(0), :], qts[h])
    for h in range(n_heads):
        update_head(h, _dot(kc_ref[...], qts[h]), vtc_ref[...])
    if n_lat_chunks:
        def body(t, _):
            c0 = 2 * t
            step(0, 1, c0 + 1, c0)
            step(1, 0, jnp.minimum(c0 + 2, n_lat_chunks - 1), c0 + 1)
            return 0

        lax.fori_loop(0, n_lat_chunks // 2, body, 0)

    outs_t = []
    for h in range(n_heads):
        acc = acc_refs[h][...]
        outs_t.append(acc[:ATTN_HEAD_DIM] * (1.0 / acc[ATTN_HEAD_DIM:ATTN_HEAD_DIM + 1]))
    for pair in range(n_heads // 2):
        o_t = jnp.concatenate([outs_t[2 * pair], outs_t[2 * pair + 1]], axis=0)
        o_ref[:, pair * LANES:(pair + 1) * LANES] = o_t.T.astype(BF16)


def _value_rows(geom, v, ck):
    n_kv, hd = ATTN_KV_HEADS, ATTN_HEAD_DIM
    n_rows = v.shape[0]
    vt = v.T.reshape(n_kv, hd, n_rows)
    vt = jnp.concatenate([vt, jnp.ones((n_kv, BF16_ROWS, n_rows), v.dtype)], axis=1)
    lat = vt[:, :, :geom.rows_lat].reshape(n_kv, VT_ROWS, geom.B, geom.L // ck, ck)
    ctx = vt[:, :, geom.rows_lat:].reshape(n_kv, VT_ROWS, geom.B, geom.C)
    return lat.transpose(2, 0, 3, 1, 4), ctx.transpose(2, 0, 1, 3)


def _flash_t(geom, q, k2, vt_lat, vt_ctx, tq, ck, latent_queries, o_prev=None):
    b, l, c = geom.B, geom.L, geom.C
    gw = 4 * ATTN_HEAD_DIM
    ctx_block0 = geom.rows_lat // c
    chunks = l // ck
    if latent_queries:
        nq = l // tq
        q_spec = pl.BlockSpec((tq, gw), lambda bi, g, i: (bi * nq + i, g))
        kl_rows, n_lat_chunks = l, chunks
        kl_map = lambda bi, g, i: (bi, g)
        vtl_map = lambda bi, g, i: (bi, g, 0, 0, 0)
    else:
        nq = 1
        q_spec = pl.BlockSpec((tq, gw), lambda bi, g, i: (ctx_block0 + bi, g))
        kl_rows, n_lat_chunks = ck, 0
        kl_map = lambda bi, g, i: (0, g)
        vtl_map = lambda bi, g, i: (0, g, 0, 0, 0)
    kl_spec = pl.BlockSpec((kl_rows, LANES), kl_map)
    vtl_spec = pl.BlockSpec((None, None, chunks, VT_ROWS, ck), vtl_map)
    kc_spec = pl.BlockSpec((c, LANES), lambda bi, g, i: (ctx_block0 + bi, g))
    vtc_spec = pl.BlockSpec((None, None, VT_ROWS, c), lambda bi, g, i: (bi, g, 0, 0))
    args = [q, k2, vt_lat, k2, vt_ctx]
    in_specs = [q_spec, kl_spec, vtl_spec, kc_spec, vtc_spec]
    aliases = {}
    if o_prev is not None:
        args.append(o_prev)
        in_specs.append(pl.BlockSpec(memory_space=pl.ANY))
        aliases = {len(args) - 1: 0}
    return pl.pallas_call(
        functools.partial(_flash_t_kernel, n_lat_chunks=n_lat_chunks, ck=ck),
        grid=(b, ATTN_KV_HEADS, nq),
        in_specs=in_specs,
        out_specs=q_spec,
        out_shape=jax.ShapeDtypeStruct(q.shape, BF16),
        input_output_aliases=aliases,
        scratch_shapes=(
            [pltpu.VMEM((ck, tq) if n_lat_chunks else (SUBLANES, LANES), F32)] * 8
            + [pltpu.VMEM((1, tq), F32)] * 4 + [pltpu.VMEM((VT_ROWS, tq), F32)] * 4),
        compiler_params=_params(3),
        name="flash_lat" if latent_queries else "flash_ctx",
    )(*args)


def _rope_tables(seq, tm):
    rows = seq // GRID_W
    row = jnp.repeat(jnp.arange(rows, dtype=F32), GRID_W)
    col = jnp.tile(jnp.arange(GRID_W, dtype=F32), rows)
    axis_dim = ATTN_HEAD_DIM // 2
    inv_freq = ROPE_THETA ** (-jnp.arange(0, axis_dim, 2, dtype=F32) / axis_dim)
    ang = jnp.concatenate([row[:, None] * inv_freq, col[:, None] * inv_freq], axis=-1)
    cos = jnp.repeat(jnp.cos(ang), 2, axis=-1)
    sign = jnp.tile(jnp.array([-1.0, 1.0], F32), ATTN_HEAD_DIM // 2)
    sin = jnp.repeat(jnp.sin(ang), 2, axis=-1) * sign
    reps = LANES // ATTN_HEAD_DIM
    cos = jnp.concatenate([jnp.tile(cos, (1, reps)), jnp.ones((tm, LANES), F32)], axis=0)
    sin = jnp.concatenate([jnp.tile(sin, (1, reps)), jnp.zeros((tm, LANES), F32)], axis=0)
    return cos, sin


def _head_mean_matrix(n, head_dim):
    idx = jnp.arange(n) // head_dim
    return jnp.where(idx[:, None] == idx[None, :], 1.0 / head_dim, 0.0).astype(BF16)


def kernel(x, c, ctx, c_ctx, ada_w, ada_b, ln_mix_g, ln_mix_b, ln_ffn_g, ln_ffn_b, ffn_w1, ffn_w3, ffn_w2, conv_pw1_w, conv_pw1_b, conv_dw_w, conv_dw_b, conv_norm_g, conv_norm_b, conv_pw2_w, conv_pw2_b, ssm_in_w, ssm_conv_w, ssm_conv_b, ssm_a_log, ssm_dt_bias, ssm_d, ssm_norm_g, ssm_out_w, attn_qkv_w, attn_q_norm, attn_k_norm, attn_out_w):
    batch, seq, d = x.shape
    ctx_len = ctx.shape[1]
    geom = _Geom(batch, seq, ctx_len, d)
    tm = math.gcd(512, math.gcd(seq, batch * ctx_len))
    tmc = math.gcd(256, math.gcd(seq, ctx_len))
    assert batch < MOD_ROWS and seq % GRID_W == 0 and ctx_len % SSM_CHUNK == 0

    cond = jnp.zeros((MOD_ROWS, d), F32).at[:batch].set(c).at[batch].set(c_ctx)
    mods = _modulation_table(cond, ada_w, ada_b)

    h = (x.reshape(batch * seq, d), ctx.reshape(batch * ctx_len, d))
    vec = lambda a: a.reshape(1, -1)

    for i in range(DEPTH):
        last = i == DEPTH - 1
        kind, j = i % N_MIXERS, i // N_MIXERS
        rows_in = geom.rows if isinstance(h, tuple) else h.shape[0]
        n_rows = min(geom.rows_lat if (last and kind == 0) else rows_in, rows_in)
        g_mix, b_mix = vec(ln_mix_g[i]), vec(ln_mix_b[i])

        if kind == 0:
            h = _conv_layer(geom, i, h, mods, conv_pw1_w[j].astype(BF16), vec(conv_pw1_b[j]),
                            conv_dw_w[j], vec(conv_dw_b[j]), vec(conv_norm_g[j]),
                            vec(conv_norm_b[j]), conv_pw2_w[j].astype(BF16), vec(conv_pw2_b[j]),
                            g_mix, b_mix, n_rows, tmc)
        elif kind == 1:
            in_w = ssm_in_w[j]
            d_inner = ssm_norm_g.shape[1]
            n_xbc = ssm_conv_w.shape[2]
            n_heads2 = ssm_a_log.shape[1]
            wz = in_w[:, :d_inner].astype(BF16)
            wx = in_w[:, d_inner:d_inner + n_xbc].astype(BF16)
            wd = jnp.pad(in_w[:, d_inner + n_xbc:], ((0, 0), (0, LANES - n_heads2)))
            wd_hi = wd.astype(BF16)
            wd_lo = (wd - wd_hi.astype(F32)).astype(BF16)
            dt_bias = jnp.pad(ssm_dt_bias[j], (0, LANES - n_heads2)).reshape(1, LANES)
            ssm_args = (geom, i, h, mods, wz, wx, wd_hi, wd_lo, dt_bias, ssm_conv_w[j],
                        vec(ssm_conv_b[j]))
            outs = _ssm_in(*ssm_args, tm, 0, geom.rows_lat // tm)
            z, xbc, dt, dtt = _ssm_in(*ssm_args, tmc, geom.rows_lat,
                                      (n_rows - geom.rows_lat) // tmc, prev_out=outs)
            a_log = jnp.pad(ssm_a_log[j], (0, LANES - n_heads2), constant_values=NEG_BIG)
            yf, yb = _ssd(geom, xbc, dt, dtt, a_log.reshape(1, LANES), a_log.reshape(LANES, 1),
                          d_inner, True)
            d_skip = jnp.repeat(ssm_d[j], SSM_HEAD_DIM).reshape(1, d_inner)
            h = _ssm_out(geom, i, yf, yb, xbc, z, d_skip, vec(ssm_norm_g[j]),
                         ssm_out_w[j].astype(BF16), h, mods, g_mix, b_mix, n_rows, tm)
        else:
            w = attn_qkv_w[j]
            hd = ATTN_HEAD_DIM
            n_q = attn_out_w.shape[1]
            n_kv = (w.shape[1] - n_q) // 2
            wq = w[:, :n_q].astype(BF16)
            wk = w[:, n_q:n_q + n_kv].reshape(d, ATTN_KV_HEADS, 1, hd)
            wk2 = jnp.broadcast_to(wk, (d, ATTN_KV_HEADS, 2, hd)).reshape(d, 2 * n_kv)
            wv = w[:, n_q + n_kv:].reshape(d, ATTN_KV_HEADS, 1, hd)
            wv2 = jnp.concatenate([wv, jnp.zeros_like(wv)], axis=2).reshape(d, 2 * n_kv)
            gq = jnp.tile(attn_q_norm[j], n_q // hd).reshape(1, n_q)
            gk = jnp.tile(attn_k_norm[j], 2 * n_kv // hd).reshape(1, 2 * n_kv)
            cos_t, sin_t = _rope_tables(seq, tm)
            q, k2, v2 = _qkv(geom, i, h, mods, wq, wk2.astype(BF16), wv2.astype(BF16),
                             _head_mean_matrix(MXU_COLS, hd), _head_mean_matrix(MXU_COLS, hd),
                             gq, gk, cos_t, sin_t, n_rows, tm)
            tq = math.gcd(512, seq)
            ck = math.gcd(1024, seq // 2)
            o = _flash(geom, q, k2, v2, tq, ck, True)
            o = _flash(geom, q, k2, v2, ctx_len, ck, False, o_prev=o)
            h = _proj_res(geom, i, o, attn_out_w[j].astype(BF16), h, mods, g_mix, b_mix,
                          n_rows, tm)

        h = _ffn(geom, i, h, mods, ffn_w1[i].astype(BF16), ffn_w3[i].astype(BF16),
                 ffn_w2[i].astype(BF16), vec(ln_ffn_g[i]), vec(ln_ffn_b[i]), n_rows, tm)

    return h[:geom.rows_lat].reshape(batch, seq, d)
```

```python
import functools
import math

import jax
import jax.numpy as jnp
from jax import lax
from jax.experimental import pallas as pl
from jax.experimental.pallas import tpu as pltpu

F32 = jnp.float32
BF16 = jnp.bfloat16

DEPTH = 4
GRID_W = 64
N_MIXERS = 3
DEEPNORM_ALPHA = (2 * DEPTH) ** 0.25
LN_EPS = 1e-5
RMS_EPS = 1e-6
CONV_WIDTH = 31
SSM_HEAD_DIM = 64
SSM_GROUPS = 8
SSM_STATE = 128
SSM_CONV_WIDTH = 5
SSM_CHUNK = 128
ATTN_HEAD_DIM = 64
ATTN_KV_HEADS = 4
ROPE_THETA = 10000.0

LANES = 128
SUBLANES = 8
BF16_ROWS = 16
VMEM_LIMIT = 56 * 1024 * 1024

MOD_ROWS = 8
NEG_BIG = -1e30


def _dot(a, b):
    return jnp.dot(a, b, preferred_element_type=F32)


def _dot_nt(a, b):
    return lax.dot_general(a, b, (((1,), (1,)), ((), ())), preferred_element_type=F32)


def _sigmoid(x):
    return 1.0 / (1.0 + jnp.exp(-x))


def _silu(x):
    return x * _sigmoid(x)


def _layer_norm(z, g, b):
    mu = jnp.mean(z, axis=-1, keepdims=True)
    d = z - mu
    var = jnp.mean(d * d, axis=-1, keepdims=True)
    return d * lax.rsqrt(var + LN_EPS) * g + b


def _split3(a):
    a1 = a.astype(BF16)
    r1 = a - a1.astype(F32)
    a2 = r1.astype(BF16)
    a3 = (r1 - a2.astype(F32)).astype(BF16)
    return a1, a2, a3


def _const_spec(shape):
    nd = len(shape)
    return pl.BlockSpec(shape, lambda *_: (0,) * nd, pipeline_mode=pl.Buffered(1))


def _params(n_grid_axes, sem=None):
    return pltpu.CompilerParams(
        dimension_semantics=sem or ("parallel",) * n_grid_axes,
        vmem_limit_bytes=VMEM_LIMIT)


class _Geom:
    def __init__(self, batch, seq, ctx_len, d_model):
        self.B, self.L, self.C, self.D = batch, seq, ctx_len, d_model
        self.rows_lat = batch * seq
        self.rows = batch * (seq + ctx_len)

    def mod_spec(self, layer, which, tm, tile0=0):
        n_lat_tiles = self.rows_lat // tm
        tiles_per_seq = self.L // tm
        b = self.B

        def imap(t, *_):
            t = t + tile0
            row = jnp.where(t < n_lat_tiles, t // tiles_per_seq, b)
            return ((layer * MOD_ROWS + row) * 6 + which, 0, 0)

        return pl.BlockSpec((None, 1, self.D), imap)


def _mod_kernel(cond_ref, w_ref, b_ref, o_ref):
    s = _silu(cond_ref[...])
    w = w_ref[...]
    s_hi = s.astype(BF16)
    s_lo = (s - s_hi.astype(F32)).astype(BF16)
    w_hi = w.astype(BF16)
    w_lo = (w - w_hi.astype(F32)).astype(BF16)
    acc = _dot(s_hi, w_hi) + _dot(s_lo, w_hi) + _dot(s_hi, w_lo)
    o_ref[...] = acc + b_ref[...]


def _modulation_table(cond, ada_w, ada_b):
    depth, d, n = ada_w.shape
    tn = n // 4
    out = pl.pallas_call(
        _mod_kernel,
        grid=(depth, n // tn),
        in_specs=[
            pl.BlockSpec((MOD_ROWS, d), lambda l, j: (0, 0)),
            pl.BlockSpec((None, d, tn), lambda l, j: (l, 0, j)),
            pl.BlockSpec((None, 1, tn), lambda l, j: (l, 0, j)),
        ],
        out_specs=pl.BlockSpec((None, MOD_ROWS, tn), lambda l, j: (l, 0, j)),
        out_shape=jax.ShapeDtypeStruct((depth, MOD_ROWS, n), F32),
        compiler_params=_params(2),
        name="mod_table",
    )(cond, ada_w, ada_b.reshape(depth, 1, n))
    return out.reshape(depth * MOD_ROWS * 6, 1, d)


def _ffn_kernel(h_ref, sh_ref, sc_ref, gt_ref, w1_ref, w3_ref, w2_ref, g_ref, b_ref, o_ref,
                *, chunk):
    h = h_ref[...]
    u = (h * (1.0 + sc_ref[...]) + sh_ref[...]).astype(BF16)
    hidden = w1_ref.shape[1]
    acc = jnp.zeros(h.shape, F32)
    for c in range(0, hidden, chunk):
        a = _dot(u, w1_ref[:, c:c + chunk])
        b = _dot(u, w3_ref[:, c:c + chunk])
        g = (_silu(a) * b).astype(BF16)
        acc = acc + _dot(g, w2_ref[c:c + chunk, :])
    z = DEEPNORM_ALPHA * h + gt_ref[...] * acc
    o_ref[...] = _layer_norm(z, g_ref[...], b_ref[...])


def _ffn(geom, layer, h, mods, w1, w3, w2, ln_g, ln_b, n_rows, tm):
    d = geom.D
    hidden = w1.shape[1]
    row = pl.BlockSpec((tm, d), lambda t: (t, 0))
    return pl.pallas_call(
        functools.partial(_ffn_kernel, chunk=2 * LANES),
        grid=(n_rows // tm,),
        in_specs=[row, geom.mod_spec(layer, 3, tm), geom.mod_spec(layer, 4, tm),
                  geom.mod_spec(layer, 5, tm),
                  _const_spec((d, hidden)), _const_spec((d, hidden)), _const_spec((hidden, d)),
                  _const_spec((1, d)), _const_spec((1, d))],
        out_specs=row,
        out_shape=jax.ShapeDtypeStruct((n_rows, d), F32),
        compiler_params=_params(1),
        name="ffn",
    )(h, mods, mods, mods, w1, w3, w2, ln_g, ln_b)


def _proj_res_kernel(x_ref, w_ref, h_ref, gt_ref, g_ref, b_ref, o_ref):
    y = _dot(x_ref[...], w_ref[...])
    z = DEEPNORM_ALPHA * h_ref[...] + gt_ref[...] * y
    o_ref[...] = _layer_norm(z, g_ref[...], b_ref[...])


def _proj_res(geom, layer, x, w, h, mods, ln_g, ln_b, n_rows, tm):
    d = geom.D
    k = x.shape[1]
    return pl.pallas_call(
        _proj_res_kernel,
        grid=(n_rows // tm,),
        in_specs=[pl.BlockSpec((tm, k), lambda t: (t, 0)), _const_spec((k, d)),
                  pl.BlockSpec((tm, d), lambda t: (t, 0)), geom.mod_spec(layer, 2, tm),
                  _const_spec((1, d)), _const_spec((1, d))],
        out_specs=pl.BlockSpec((tm, d), lambda t: (t, 0)),
        out_shape=jax.ShapeDtypeStruct((n_rows, d), F32),
        compiler_params=_params(1),
        name="proj_res",
    )(x, w, h, mods, ln_g, ln_b)


def _seq_edges(t, tm, geom):
    n_lat_tiles = geom.rows_lat // tm
    lat_tps = geom.L // tm
    ctx_tps = max(geom.C // tm, 1)
    in_lat = t < n_lat_tiles
    pos = jnp.where(in_lat, t % lat_tps, (t - n_lat_tiles) % ctx_tps)
    tps = jnp.where(in_lat, lat_tps, ctx_tps)
    return pos == 0, pos == tps - 1


def _store_window(win_ref, cols, value, halo, tm, is_start, is_end):
    win_ref[:, cols] = value
    win_ref[0:halo, cols] = jnp.where(is_start, 0.0, value[0:halo])
    win_ref[halo + tm:, cols] = jnp.where(is_end, 0.0, value[halo + tm:])


def _halo_specs(tm, halo, n_cols, n_rows_total, tile0=0):
    per = tm // halo
    last = n_rows_total // halo - 1
    tile = lambda t: jnp.clip(t + tile0, 0, n_rows_total // tm - 1)
    prev = pl.BlockSpec((halo, n_cols), lambda t: (jnp.maximum(tile(t) * per - 1, 0), 0))
    cur = pl.BlockSpec((tm, n_cols), lambda t: (tile(t), 0))
    nxt = pl.BlockSpec((halo, n_cols), lambda t: (jnp.minimum((tile(t) + 1) * per, last), 0))
    return prev, cur, nxt


def _depthwise(win_ref, w_ref, width, first, tm, cols, emit, row_block=64):
    phases = {}
    for k in range(width):
        a, b = divmod(first + k, SUBLANES)
        phases.setdefault(b, []).append((a, k))
    for c in range(cols.start, cols.stop, LANES):
        for r in range(0, tm, row_block):
            acc = None
            for b, taps in sorted(phases.items()):
                z = None
                for a, k in taps:
                    lo = r + SUBLANES * a
                    term = (win_ref[lo:lo + row_block + SUBLANES, c:c + LANES]
                            * w_ref[k:k + 1, c:c + LANES])
                    z = term if z is None else z + term
                zb = z[b:b + row_block]
                acc = zb if acc is None else acc + zb
            emit(r, c, acc)


MXU_COLS = 2 * LANES


CONV_HALO = 16


def _conv_layer_kernel(*refs, geom, tm, split_input):
    n_in = 6 if split_input else 3
    (sh_ref, sc_ref, gt_ref, wa_ref, wg_ref, ba_ref, bg_ref, dww_ref, dwb_ref, ng_ref, nb_ref,
     w2_ref, b2_ref, g_ref, b_ref, o_ref, win_ref, cv_ref) = refs[n_in:]
    halo = CONV_HALO
    t = pl.program_id(0)
    is_start, is_end = _seq_edges(t, tm, geom)
    hwin = jnp.concatenate([r[...] for r in refs[0:3]], axis=0)
    h_tile = refs[1][...]
    if split_input:
        in_lat = t < geom.rows_lat // tm
        hwin = jnp.where(in_lat, hwin, jnp.concatenate([r[...] for r in refs[3:6]], axis=0))
        h_tile = jnp.where(in_lat, h_tile, refs[4][...])
    u = (hwin * (1.0 + sc_ref[...]) + sh_ref[...]).astype(BF16)

    def glu_block(cols):
        a = _dot(u, wa_ref[:, cols]) + ba_ref[:, cols]
        g = _dot(u, wg_ref[:, cols]) + bg_ref[:, cols]
        _store_window(win_ref, cols, a * _sigmoid(g), halo, tm, is_start, is_end)

    def emit(r, c, acc):
        cv_ref[r:r + acc.shape[0], c:c + LANES] = acc + dwb_ref[:, c:c + LANES]

    blocks = [slice(c, c + MXU_COLS) for c in range(0, geom.D, MXU_COLS)]
    glu_block(blocks[0])
    for j, cols in enumerate(blocks):
        if j + 1 < len(blocks):
            glu_block(blocks[j + 1])
        _depthwise(win_ref, dww_ref, CONV_WIDTH, halo - (CONV_WIDTH - 1) // 2, tm, cols, emit)
    v = _silu(_layer_norm(cv_ref[...], ng_ref[...], nb_ref[...])).astype(BF16)
    y = _dot(v, w2_ref[...]) + b2_ref[...]
    z = DEEPNORM_ALPHA * h_tile + gt_ref[...] * y
    o_ref[...] = _layer_norm(z, g_ref[...], b_ref[...])


def _conv_layer(geom, layer, h, mods, w1, b1, dw_w, dw_b, n_g, n_b, w2, b2, ln_g, ln_b,
                n_rows, tm):
    d = geom.D
    split_input = isinstance(h, tuple)
    if split_input:
        streams = [h[0]] * 3 + [h[1]] * 3
        stream_specs = (list(_halo_specs(tm, CONV_HALO, d, h[0].shape[0]))
                        + list(_halo_specs(tm, CONV_HALO, d, h[1].shape[0],
                                           tile0=-(h[0].shape[0] // tm))))
    else:
        streams = [h] * 3
        stream_specs = list(_halo_specs(tm, CONV_HALO, d, h.shape[0]))
    half = lambda j: pl.BlockSpec((d, d), lambda t: (0, j), pipeline_mode=pl.Buffered(1))
    bhalf = lambda j: pl.BlockSpec((1, d), lambda t: (0, j), pipeline_mode=pl.Buffered(1))
    vec = _const_spec((1, d))
    return pl.pallas_call(
        functools.partial(_conv_layer_kernel, geom=geom, tm=tm, split_input=split_input),
        grid=(n_rows // tm,),
        in_specs=stream_specs + [
            geom.mod_spec(layer, 0, tm), geom.mod_spec(layer, 1, tm),
            geom.mod_spec(layer, 2, tm), half(0), half(1), bhalf(0), bhalf(1),
            _const_spec((CONV_WIDTH, d)), vec, vec, vec, _const_spec((d, d)), vec, vec, vec],
        out_specs=pl.BlockSpec((tm, d), lambda t: (t, 0)),
        out_shape=jax.ShapeDtypeStruct((n_rows, d), F32),
        scratch_shapes=[pltpu.VMEM((tm + 2 * CONV_HALO, d), F32), pltpu.VMEM((tm, d), F32)],
        compiler_params=_params(1),
        name="conv_layer",
    )(*streams, mods, mods, mods, w1, w1, b1, b1, dw_w, dw_b, n_g, n_b, w2, b2, ln_g, ln_b)


SSM_HALO = 8


def _ssm_in_kernel(hp_ref, hc_ref, hn_ref, sh_ref, sc_ref, wz_ref, wx_ref, wdh_ref, wdl_ref,
                   dtb_ref, cw_ref, cb_ref, *rest, geom, tm, tile0, col_chunk):
    z_ref, x_ref, dt_ref, dtt_ref, win_ref = rest[-5:]
    halo = SSM_HALO
    is_start, is_end = _seq_edges(pl.program_id(0) + tile0, tm, geom)
    hwin = jnp.concatenate([hp_ref[...], hc_ref[...], hn_ref[...]], axis=0)
    uf = hwin * (1.0 + sc_ref[...]) + sh_ref[...]
    u = uf.astype(BF16)
    u_lo = (uf - u.astype(F32)).astype(BF16)
    own = slice(halo, halo + tm)

    def project_x(cols):
        _store_window(win_ref, cols, _dot(u, wx_ref[:, cols]), halo, tm, is_start, is_end)

    def emit(r, c, acc):
        col = slice(c, c + LANES)
        x_ref[r:r + acc.shape[0], col] = _silu(acc + cb_ref[:, col]).astype(BF16)

    def conv_x(cols):
        _depthwise(win_ref, cw_ref, SSM_CONV_WIDTH, halo - (SSM_CONV_WIDTH - 1) // 2, tm, cols,
                   emit)

    def project_z(cols):
        z_ref[:, cols] = _dot(u, wz_ref[:, cols])[own].astype(BF16)

    def project_dt():
        d = (_dot(u, wdh_ref[...]) + _dot(u_lo, wdh_ref[...]) + _dot(u, wdl_ref[...]))[own]
        d = d + dtb_ref[...]
        dt = jnp.maximum(d, 0.0) + jnp.log1p(jnp.exp(-jnp.abs(d)))
        dt_ref[...] = dt
        dtt_ref[...] = dt.T

    x_blocks = [slice(c, c + col_chunk) for c in range(0, wx_ref.shape[1], col_chunk)]
    z_blocks = [slice(c, c + col_chunk) for c in range(0, wz_ref.shape[1], col_chunk)]
    extra = [functools.partial(project_z, cols) for cols in z_blocks] + [project_dt]
    per_conv = -(-len(extra) // len(x_blocks))
    project_x(x_blocks[0])
    for j, cols in enumerate(x_blocks):
        if j + 1 < len(x_blocks):
            project_x(x_blocks[j + 1])
        for f in extra[j * per_conv:(j + 1) * per_conv]:
            f()
        conv_x(cols)


def _ssm_in(geom, layer, h, mods, wz, wx, wd_hi, wd_lo, dt_bias, conv_w, conv_b, tm, row0,
            n_tiles, prev_out=None):
    d = geom.D
    nz, nx = wz.shape[1], wx.shape[1]
    n_rows = h.shape[0]
    tile0 = row0 // tm
    col_chunk = MXU_COLS
    prev, cur, nxt = _halo_specs(tm, SSM_HALO, d, n_rows, tile0)
    args = [h, h, h, mods, mods, wz, wx, wd_hi, wd_lo, dt_bias, conv_w, conv_b]
    in_specs = [prev, cur, nxt, geom.mod_spec(layer, 0, tm, tile0),
                geom.mod_spec(layer, 1, tm, tile0),
                _const_spec((d, nz)), _const_spec((d, nx)),
                _const_spec((d, LANES)), _const_spec((d, LANES)), _const_spec((1, LANES)),
                _const_spec((SSM_CONV_WIDTH, nx)), _const_spec((1, nx))]
    aliases = {}
    if prev_out is not None:
        for o in prev_out:
            aliases[len(args)] = len(aliases)
            args.append(o)
            in_specs.append(pl.BlockSpec(memory_space=pl.ANY))
    return pl.pallas_call(
        functools.partial(_ssm_in_kernel, geom=geom, tm=tm, tile0=tile0, col_chunk=col_chunk),
        grid=(n_tiles,),
        in_specs=in_specs,
        out_specs=[pl.BlockSpec((tm, nz), lambda t: (t + tile0, 0)),
                   pl.BlockSpec((tm, nx), lambda t: (t + tile0, 0)),
                   pl.BlockSpec((tm, LANES), lambda t: (t + tile0, 0)),
                   pl.BlockSpec((LANES, tm), lambda t: (0, t + tile0))],
        out_shape=[jax.ShapeDtypeStruct((n_rows, nz), BF16),
                   jax.ShapeDtypeStruct((n_rows, nx), BF16),
                   jax.ShapeDtypeStruct((n_rows, LANES), F32),
                   jax.ShapeDtypeStruct((LANES, n_rows), F32)],
        input_output_aliases=aliases,
        scratch_shapes=[pltpu.VMEM((tm + 2 * SSM_HALO, nx), F32)],
        compiler_params=_params(1),
        name="ssm_in",
    )(*args)


def _ssd_direction(x_ref, b_ref, c_ref, dt_ref, dtt_ref, alog_row_ref, alog_col_ref, st_ref,
                   y_ref, *, backward):
    q = SSM_CHUNK
    n_state = SSM_STATE
    hpg = 4
    ii = lax.broadcasted_iota(jnp.int32, (q, q), 0)
    jj = lax.broadcasted_iota(jnp.int32, (q, q), 1)
    mask = (ii <= jj) if backward else (ii >= jj)
    tri = mask.astype(BF16)
    tri_t = ((ii >= jj) if backward else (ii <= jj)).astype(BF16)

    dt = dt_ref[...]
    dtt = dtt_ref[...]
    a_col = dt * (-jnp.exp(alog_row_ref[...]))
    a_row = dtt * (-jnp.exp(alog_col_ref[...]))
    cum_col = sum(_dot(tri, p) for p in _split3(a_col))
    cum_row = sum(_dot(p, tri_t) for p in _split3(a_row))
    last = 0 if backward else q - 1
    cum_last_col = cum_row[:, last:last + 1]
    w_row = jnp.exp(cum_last_col - cum_row) * dtt
    e_last_row = jnp.exp(cum_col[last:last + 1, :])

    lane = lax.broadcasted_iota(jnp.int32, (q, LANES), 1)
    low_half = lane < SSM_HEAD_DIM
    base = 32 * (1 if backward else 0)

    def group(g):
        bg = b_ref[:, g * n_state:(g + 1) * n_state]
        cg = c_ref[:, g * n_state:(g + 1) * n_state]
        cb = _dot_nt(cg, bg)
        bt = bg.astype(F32).T
        cg32 = cg.astype(F32)
        yield
        for pair in range(hpg // 2):
            slot = g * (hpg // 2) + pair
            xp = x_ref[:, slot * LANES:(slot + 1) * LANES]
            hs = st_ref[slot]
            rhs = jnp.concatenate([xp, hs.astype(BF16)], axis=0)
            ys, ss, els = [], [], []
            for r in range(2):
                hl = base + g * hpg + pair * 2 + r
                cc = jnp.broadcast_to(cum_col[:, hl:hl + 1], (q, LANES))
                cr = cum_row[hl:hl + 1, :]
                seg = jnp.exp(jnp.where(mask, cc - cr, NEG_BIG))
                m = (cb * seg * dtt[hl:hl + 1, :]).astype(BF16)
                ce = (cg32 * jnp.exp(cc)).astype(BF16)
                ys.append(_dot(jnp.concatenate([m, ce], axis=1), rhs))
                bts = (bt * w_row[hl:hl + 1, :]).astype(BF16)
                ss.append(_dot(bts, xp))
                els.append(e_last_row[:, hl:hl + 1])
            y_ref[:, slot * LANES:(slot + 1) * LANES] = jnp.where(
                low_half, ys[0], ys[1]).astype(y_ref.dtype)
            low_n = low_half[:n_state]
            decay = jnp.where(low_n, els[0], els[1])
            st_ref[slot] = hs * decay + jnp.where(low_n, ss[0], ss[1])
            yield

    return group


def _ssd_kernel(xf_ref, bf_ref, cf_ref, dtf_ref, dttf_ref,
                xb_ref, bb_ref, cb_ref, dtb_ref, dttb_ref,
                alog_row_ref, alog_col_ref, yf_ref, yb_ref, stf_ref, stb_ref):
    @pl.when(pl.program_id(1) == 0)
    def _():
        stf_ref[...] = jnp.zeros(stf_ref.shape, F32)
        stb_ref[...] = jnp.zeros(stb_ref.shape, F32)

    fwd = _ssd_direction(xf_ref, bf_ref, cf_ref, dtf_ref, dttf_ref, alog_row_ref, alog_col_ref,
                         stf_ref, yf_ref, backward=False)
    bwd = _ssd_direction(xb_ref, bb_ref, cb_ref, dtb_ref, dttb_ref, alog_row_ref, alog_col_ref,
                         stb_ref, yb_ref, backward=True)
    for g in range(0, SSM_GROUPS, 2):
        for _ in zip(fwd(g), bwd(g), fwd(g + 1), bwd(g + 1)):
            pass


def _ssd(geom, xbc, dt, dtt, alog_row, alog_col, d_inner, with_ctx_out):
    q = SSM_CHUNK
    n_bc = SSM_GROUPS * SSM_STATE
    lat_chunks = geom.L // q
    ctx_chunks = geom.C // q
    lat_total = geom.rows_lat // q
    steps = lat_chunks + ctx_chunks

    def fwd_chunk(b, s):
        return jnp.where(s < ctx_chunks, lat_total + b * ctx_chunks + s,
                         b * lat_chunks + (s - ctx_chunks))

    def bwd_chunk(b, s):
        return jnp.where(s < ctx_chunks, lat_total + b * ctx_chunks + (ctx_chunks - 1 - s),
                         b * lat_chunks + (lat_chunks - 1 - (s - ctx_chunks)))

    def specs(chunk):
        return [pl.BlockSpec((q, d_inner), lambda b, s: (chunk(b, s), 0)),
                pl.BlockSpec((q, n_bc), lambda b, s: (chunk(b, s), d_inner // n_bc)),
                pl.BlockSpec((q, n_bc), lambda b, s: (chunk(b, s), d_inner // n_bc + 1)),
                pl.BlockSpec((q, LANES), lambda b, s: (chunk(b, s), 0)),
                pl.BlockSpec((LANES, q), lambda b, s: (0, chunk(b, s)))]

    n_rows = geom.rows
    y_shape = jax.ShapeDtypeStruct((n_rows, d_inner), BF16)
    state = pltpu.VMEM((SSM_GROUPS * 2, SSM_STATE, LANES), F32)
    return pl.pallas_call(
        _ssd_kernel,
        grid=(geom.B, steps),
        in_specs=specs(fwd_chunk) + specs(bwd_chunk)
        + [pl.BlockSpec((1, LANES), lambda b, s: (0, 0)),
           pl.BlockSpec((LANES, 1), lambda b, s: (0, 0))],
        out_specs=[pl.BlockSpec((q, d_inner), lambda b, s: (fwd_chunk(b, s), 0)),
                   pl.BlockSpec((q, d_inner), lambda b, s: (bwd_chunk(b, s), 0))],
        out_shape=[y_shape, y_shape],
        scratch_shapes=[state, state],
        compiler_params=_params(2, ("parallel", "arbitrary")),
        name="ssd_scan",
    )(xbc, xbc, xbc, dt, dtt, xbc, xbc, xbc, dt, dtt, alog_row, alog_col)


def _ssm_out_kernel(yf_ref, yb_ref, x_ref, z_ref, dsk_ref, ng_ref, w_ref, h_ref, gt_ref,
                    g_ref, b_ref, o_ref):
    d_inner = yf_ref.shape[1]
    gw = d_inner // SSM_GROUPS
    acc = jnp.zeros(o_ref.shape, F32)
    for c in range(0, d_inner, gw):
        sl = slice(c, c + gw)
        y = (yf_ref[:, sl].astype(F32) + yb_ref[:, sl].astype(F32)
             + dsk_ref[:, sl] * x_ref[:, sl].astype(F32))
        gt = y * _silu(z_ref[:, sl].astype(F32))
        ms = jnp.mean(gt * gt, axis=-1, keepdims=True)
        gn = (gt * lax.rsqrt(ms + RMS_EPS) * ng_ref[:, sl]).astype(BF16)
        acc = acc + _dot(gn, w_ref[sl, :])
    zz = DEEPNORM_ALPHA * h_ref[...] + gt_ref[...] * acc
    o_ref[...] = _layer_norm(zz, g_ref[...], b_ref[...])


def _ssm_out(geom, layer, yf, yb, xbc, z, d_skip, norm_g, out_w, h, mods, ln_g, ln_b,
             n_rows, tm):
    d = geom.D
    d_inner = z.shape[1]
    wide = pl.BlockSpec((tm, d_inner), lambda t: (t, 0))
    row = pl.BlockSpec((tm, d), lambda t: (t, 0))
    return pl.pallas_call(
        _ssm_out_kernel,
        grid=(n_rows // tm,),
        in_specs=[wide, wide, wide, wide, _const_spec((1, d_inner)), _const_spec((1, d_inner)),
                  _const_spec((d_inner, d)), row, geom.mod_spec(layer, 2, tm),
                  _const_spec((1, d)), _const_spec((1, d))],
        out_specs=row,
        out_shape=jax.ShapeDtypeStruct((n_rows, d), F32),
        compiler_params=_params(1),
        name="ssm_out",
    )(yf, yb, xbc, z, d_skip, norm_g, out_w, h, mods, ln_g, ln_b)


def _qkv_kernel(h_ref, sh_ref, sc_ref, wq_ref, wk_ref, wv_ref, pq_ref, pk_ref, gq_ref, gk_ref,
                cos_ref, sin_ref, q_ref, k_ref, v_ref):
    u = (h_ref[...] * (1.0 + sc_ref[...]) + sh_ref[...]).astype(BF16)
    cos = cos_ref[...]
    sin = sin_ref[...]

    def norm_rope(x, p_ref, g_ref):
        n = x.shape[1]
        sq = (x * x).astype(BF16)
        ms = jnp.concatenate([_dot(sq[:, c:c + MXU_COLS], p_ref[...])
                              for c in range(0, n, MXU_COLS)], axis=1)
        xn = x * lax.rsqrt(ms + RMS_EPS) * g_ref[...]
        lane = lax.broadcasted_iota(jnp.int32, xn.shape, 1)
        partner = jnp.where(lane % 2 == 0, pltpu.roll(xn, n - 1, axis=1),
                            pltpu.roll(xn, 1, axis=1))
        reps = n // cos.shape[1]
        return xn * jnp.tile(cos, (1, reps)) + partner * jnp.tile(sin, (1, reps))

    q = norm_rope(_dot(u, wq_ref[...]), pq_ref, gq_ref)
    q_ref[...] = (q * (ATTN_HEAD_DIM ** -0.5 * math.log2(math.e))).astype(BF16)
    k_ref[...] = norm_rope(_dot(u, wk_ref[...]), pk_ref, gk_ref).astype(BF16)
    v = _dot(u, wv_ref[...])
    lane = lax.broadcasted_iota(jnp.int32, v.shape, 1)
    v_ref[...] = jnp.where(lane % LANES < ATTN_HEAD_DIM, v, 1.0).astype(BF16)


def _qkv(geom, layer, h, mods, wq, wk2, wv, pq, pk, gq, gk, cos_t, sin_t, n_rows, tm):
    d = geom.D
    nq, nk, nv = wq.shape[1], wk2.shape[1], wv.shape[1]
    n_lat_tiles = geom.rows_lat // tm
    tps = geom.L // tm
    table = pl.BlockSpec((tm, LANES), lambda t: (jnp.where(t < n_lat_tiles, t % tps, tps), 0))
    return pl.pallas_call(
        _qkv_kernel,
        grid=(n_rows // tm,),
        in_specs=[pl.BlockSpec((tm, d), lambda t: (t, 0)),
                  geom.mod_spec(layer, 0, tm), geom.mod_spec(layer, 1, tm),
                  _const_spec((d, nq)), _const_spec((d, nk)), _const_spec((d, nv)),
                  _const_spec((MXU_COLS, MXU_COLS)), _const_spec((MXU_COLS, MXU_COLS)),
                  _const_spec((1, nq)), _const_spec((1, nk)), table, table],
        out_specs=[pl.BlockSpec((tm, nq), lambda t: (t, 0)),
                   pl.BlockSpec((tm, nk), lambda t: (t, 0)),
                   pl.BlockSpec((tm, nv), lambda t: (t, 0))],
        out_shape=[jax.ShapeDtypeStruct((n_rows, nq), BF16),
                   jax.ShapeDtypeStruct((n_rows, nk), BF16),
                   jax.ShapeDtypeStruct((n_rows, nv), BF16)],
        compiler_params=_params(1),
        name="attn_qkv",
    )(h, mods, mods, wq, wk2, wv, pq, pk, gq, gk, cos_t, sin_t)


def _flash_kernel(q_ref, kl_ref, vl_ref, kc_ref, vc_ref, *rest, n_lat_chunks, ck):
    n_heads = 4
    n_scratch = 4 * n_heads
    o_ref = rest[-n_scratch - 1]
    s_refs = (rest[-n_scratch:-3 * n_heads], rest[-3 * n_heads:-2 * n_heads])
    m_refs = rest[-2 * n_heads:-n_heads]
    acc_refs = rest[-n_heads:]
    tq = q_ref.shape[0]
    lane = lax.broadcasted_iota(jnp.int32, (tq, LANES), 1)
    low_half = lane < ATTN_HEAD_DIM

    qms = []
    for pair in range(n_heads // 2):
        q2 = q_ref[:, pair * LANES:(pair + 1) * LANES].astype(F32)
        qms.append(jnp.where(low_half, q2, 0.0).astype(BF16))
        qms.append(jnp.where(low_half, 0.0, q2).astype(BF16))

    def update_head(h, s, vc):
        m = m_refs[h][...]
        row_max = jnp.max(s, axis=-1, keepdims=True)
        m_new = jnp.maximum(m, jnp.broadcast_to(row_max, m.shape))
        m_wide = jnp.concatenate([m_new] * (s.shape[1] // LANES), axis=1)
        p = jnp.exp2(s - m_wide).astype(BF16)
        acc_refs[h][...] = jnp.exp2(m - m_new) * acc_refs[h][...] + _dot(p, vc)
        m_refs[h][...] = m_new

    def lat_rows(chunk):
        if isinstance(chunk, int):
            return pl.ds(chunk * ck, ck)
        return pl.ds(pl.multiple_of(chunk * ck, ck), ck)

    def step(cur, nxt, next_chunk, cur_chunk):
        kc = kl_ref[lat_rows(next_chunk), :]
        vc = vl_ref[lat_rows(cur_chunk), :]
        for h in range(n_heads):
            s_refs[nxt][h][...] = _dot_nt(qms[h], kc)
            update_head(h, s_refs[cur][h][...], vc)

    for h in range(n_heads):
        m_refs[h][...] = jnp.full((tq, LANES), NEG_BIG, F32)
        acc_refs[h][...] = jnp.zeros((tq, LANES), F32)
    n_ctx = kc_ref.shape[0]
    if n_lat_chunks:
        for h in range(n_heads):
            s_refs[0][h][...] = _dot_nt(qms[h], kl_ref[lat_rows(0), :])

        def body(t, _):
            c0 = 2 * t
            step(0, 1, c0 + 1, c0)
            step(1, 0, c0 + 2, c0 + 1)
            return 0

        lax.fori_loop(0, n_lat_chunks // 2 - 1, body, 0)
        step(0, 1, n_lat_chunks - 1, n_lat_chunks - 2)
        vc = vl_ref[lat_rows(n_lat_chunks - 1), :]
        for h in range(n_heads):
            s_refs[0][h][:, :n_ctx] = _dot_nt(qms[h], kc_ref[...])
            update_head(h, s_refs[1][h][...], vc)
        for h in range(n_heads):
            update_head(h, s_refs[0][h][:, :n_ctx], vc_ref[...])
    else:
        for h in range(n_heads):
            update_head(h, _dot_nt(qms[h], kc_ref[...]), vc_ref[...])

    accs = [r[...] for r in acc_refs]
    outs = [acc * (1.0 / pltpu.roll(acc, ATTN_HEAD_DIM, axis=1)) for acc in accs]
    for pair in range(n_heads // 2):
        o_pair = jnp.where(low_half, outs[2 * pair],
                           pltpu.roll(outs[2 * pair + 1], ATTN_HEAD_DIM, axis=1))
        o_ref[:, pair * LANES:(pair + 1) * LANES] = o_pair.astype(BF16)


def _flash(geom, q, k2, v2, tq, ck, latent_queries, o_prev=None):
    b, l, c = geom.B, geom.L, geom.C
    gw = 4 * ATTN_HEAD_DIM
    ctx_block0 = geom.rows_lat // c
    if latent_queries:
        nq = l // tq
        q_spec = pl.BlockSpec((tq, gw), lambda bi, g, i: (bi * nq + i, g))
        kl_rows, n_lat_chunks = l, l // ck
        kl_map = lambda bi, g, i: (bi, g)
    else:
        nq = 1
        q_spec = pl.BlockSpec((tq, gw), lambda bi, g, i: (ctx_block0 + bi, g))
        kl_rows, n_lat_chunks = ck, 0
        kl_map = lambda bi, g, i: (0, g)
    kl_spec = pl.BlockSpec((kl_rows, LANES), kl_map)
    kc_spec = pl.BlockSpec((c, LANES), lambda bi, g, i: (ctx_block0 + bi, g))
    args = [q, k2, v2, k2, v2]
    in_specs = [q_spec, kl_spec, kl_spec, kc_spec, kc_spec]
    aliases = {}
    if o_prev is not None:
        args.append(o_prev)
        in_specs.append(pl.BlockSpec(memory_space=pl.ANY))
        aliases = {len(args) - 1: 0}
    return pl.pallas_call(
        functools.partial(_flash_kernel, n_lat_chunks=n_lat_chunks, ck=ck),
        grid=(b, ATTN_KV_HEADS, nq),
        in_specs=in_specs,
        out_specs=q_spec,
        out_shape=jax.ShapeDtypeStruct(q.shape, BF16),
        input_output_aliases=aliases,
        scratch_shapes=(
            [pltpu.VMEM((tq, ck) if n_lat_chunks else (SUBLANES, LANES), F32)] * 8
            + [pltpu.VMEM((tq, LANES), F32)] * 8),
        compiler_params=_params(3),
        name="flash_lat" if latent_queries else "flash_ctx",
    )(*args)


def _rope_tables(seq, tm):
    rows = seq // GRID_W
    row = jnp.repeat(jnp.arange(rows, dtype=F32), GRID_W)
    col = jnp.tile(jnp.arange(GRID_W, dtype=F32), rows)
    axis_dim = ATTN_HEAD_DIM // 2
    inv_freq = ROPE_THETA ** (-jnp.arange(0, axis_dim, 2, dtype=F32) / axis_dim)
    ang = jnp.concatenate([row[:, None] * inv_freq, col[:, None] * inv_freq], axis=-1)
    cos = jnp.repeat(jnp.cos(ang), 2, axis=-1)
    sign = jnp.tile(jnp.array([-1.0, 1.0], F32), ATTN_HEAD_DIM // 2)
    sin = jnp.repeat(jnp.sin(ang), 2, axis=-1) * sign
    reps = LANES // ATTN_HEAD_DIM
    cos = jnp.concatenate([jnp.tile(cos, (1, reps)), jnp.ones((tm, LANES), F32)], axis=0)
    sin = jnp.concatenate([jnp.tile(sin, (1, reps)), jnp.zeros((tm, LANES), F32)], axis=0)
    return cos, sin


def _head_mean_matrix(n, head_dim):
    idx = jnp.arange(n) // head_dim
    return jnp.where(idx[:, None] == idx[None, :], 1.0 / head_dim, 0.0).astype(BF16)


def kernel(x, c, ctx, c_ctx, ada_w, ada_b, ln_mix_g, ln_mix_b, ln_ffn_g, ln_ffn_b, ffn_w1, ffn_w3, ffn_w2, conv_pw1_w, conv_pw1_b, conv_dw_w, conv_dw_b, conv_norm_g, conv_norm_b, conv_pw2_w, conv_pw2_b, ssm_in_w, ssm_conv_w, ssm_conv_b, ssm_a_log, ssm_dt_bias, ssm_d, ssm_norm_g, ssm_out_w, attn_qkv_w, attn_q_norm, attn_k_norm, attn_out_w):
    batch, seq, d = x.shape
    ctx_len = ctx.shape[1]
    geom = _Geom(batch, seq, ctx_len, d)
    tm = math.gcd(512, math.gcd(seq, batch * ctx_len))
    tmc = math.gcd(256, math.gcd(seq, ctx_len))
    assert batch < MOD_ROWS and seq % GRID_W == 0 and ctx_len % SSM_CHUNK == 0

    cond = jnp.zeros((MOD_ROWS, d), F32).at[:batch].set(c).at[batch].set(c_ctx)
    mods = _modulation_table(cond, ada_w, ada_b)

    h = (x.reshape(batch * seq, d), ctx.reshape(batch * ctx_len, d))
    vec = lambda a: a.reshape(1, -1)

    for i in range(DEPTH):
        last = i == DEPTH - 1
        kind, j = i % N_MIXERS, i // N_MIXERS
        rows_in = geom.rows if isinstance(h, tuple) else h.shape[0]
        n_rows = min(geom.rows_lat if (last and kind == 0) else rows_in, rows_in)
        g_mix, b_mix = vec(ln_mix_g[i]), vec(ln_mix_b[i])

        if kind == 0:
            h = _conv_layer(geom, i, h, mods, conv_pw1_w[j].astype(BF16), vec(conv_pw1_b[j]),
                            conv_dw_w[j], vec(conv_dw_b[j]), vec(conv_norm_g[j]),
                            vec(conv_norm_b[j]), conv_pw2_w[j].astype(BF16), vec(conv_pw2_b[j]),
                            g_mix, b_mix, n_rows, tmc)
        elif kind == 1:
            in_w = ssm_in_w[j]
            d_inner = ssm_norm_g.shape[1]
            n_xbc = ssm_conv_w.shape[2]
            n_heads2 = ssm_a_log.shape[1]
            wz = in_w[:, :d_inner].astype(BF16)
            wx = in_w[:, d_inner:d_inner + n_xbc].astype(BF16)
            wd = jnp.pad(in_w[:, d_inner + n_xbc:], ((0, 0), (0, LANES - n_heads2)))
            wd_hi = wd.astype(BF16)
            wd_lo = (wd - wd_hi.astype(F32)).astype(BF16)
            dt_bias = jnp.pad(ssm_dt_bias[j], (0, LANES - n_heads2)).reshape(1, LANES)
            ssm_args = (geom, i, h, mods, wz, wx, wd_hi, wd_lo, dt_bias, ssm_conv_w[j],
                        vec(ssm_conv_b[j]))
            outs = _ssm_in(*ssm_args, tm, 0, geom.rows_lat // tm)
            z, xbc, dt, dtt = _ssm_in(*ssm_args, tmc, geom.rows_lat,
                                      (n_rows - geom.rows_lat) // tmc, prev_out=outs)
            a_log = jnp.pad(ssm_a_log[j], (0, LANES - n_heads2), constant_values=NEG_BIG)
            yf, yb = _ssd(geom, xbc, dt, dtt, a_log.reshape(1, LANES), a_log.reshape(LANES, 1),
                          d_inner, True)
            d_skip = jnp.repeat(ssm_d[j], SSM_HEAD_DIM).reshape(1, d_inner)
            h = _ssm_out(geom, i, yf, yb, xbc, z, d_skip, vec(ssm_norm_g[j]),
                         ssm_out_w[j].astype(BF16), h, mods, g_mix, b_mix, n_rows, tm)
        else:
            w = attn_qkv_w[j]
            hd = ATTN_HEAD_DIM
            n_q = attn_out_w.shape[1]
            n_kv = (w.shape[1] - n_q) // 2
            wq = w[:, :n_q].astype(BF16)
            wk = w[:, n_q:n_q + n_kv].reshape(d, ATTN_KV_HEADS, 1, hd)
            wk2 = jnp.broadcast_to(wk, (d, ATTN_KV_HEADS, 2, hd)).reshape(d, 2 * n_kv)
            wv = w[:, n_q + n_kv:].reshape(d, ATTN_KV_HEADS, 1, hd)
            wv2 = jnp.concatenate([wv, jnp.zeros_like(wv)], axis=2).reshape(d, 2 * n_kv)
            gq = jnp.tile(attn_q_norm[j], n_q // hd).reshape(1, n_q)
            gk = jnp.tile(attn_k_norm[j], 2 * n_kv // hd).reshape(1, 2 * n_kv)
            cos_t, sin_t = _rope_tables(seq, tm)
            q, k2, v2 = _qkv(geom, i, h, mods, wq, wk2.astype(BF16), wv2.astype(BF16),
                             _head_mean_matrix(MXU_COLS, hd), _head_mean_matrix(MXU_COLS, hd),
                             gq, gk, cos_t, sin_t, n_rows, tm)
            tq = math.gcd(512, seq)
            ck = math.gcd(2048, seq // 2)
            o = _flash(geom, q, k2, v2, tq, ck, True)
            o = _flash(geom, q, k2, v2, ctx_len, ck, False, o_prev=o)
            h = _proj_res(geom, i, o, attn_out_w[j].astype(BF16), h, mods, g_mix, b_mix,
                          n_rows, tm)

        h = _ffn(geom, i, h, mods, ffn_w1[i].astype(BF16), ffn_w3[i].astype(BF16),
                 ffn_w2[i].astype(BF16), vec(ln_ffn_g[i]), vec(ln_ffn_b[i]), n_rows, tm)

    return h[:geom.rows_lat].reshape(batch, seq, d)
```
